```python
import math
import jax
import jax.numpy as jnp
from jax import lax
import numpy as np

D_MODEL = 2048
BATCH = 2
SEQ = 4096
DEPTH = 1

ATTN_HEADS = 8
ATTN_HEAD_DIM = 64
Q_COLS = 2 * ATTN_HEADS * ATTN_HEAD_DIM
K_COLS = 2 * ATTN_HEADS * ATTN_HEAD_DIM
V_COLS = ATTN_HEADS * 2 * ATTN_HEAD_DIM
HYENA_WIDTH = D_MODEL // 2
HYENA_ORDER = 2
HY_COLS = (HYENA_ORDER + 1) * HYENA_WIDTH
GATE_COLS = 2 * D_MODEL
IN_COLS = Q_COLS + K_COLS + V_COLS + HY_COLS + GATE_COLS
FILTER_ORDER = 64
FILTER_EMB_DIM = 33
FILTER_BANDS = (FILTER_EMB_DIM - 1) // 2
FAST_DECAY_PCT = 0.3
SLOW_DECAY_PCT = 1.5
DECAY_TARGET = 1e-2
D_FF = 5632
CONV_WIDTH = 3
N_BUCKETS = 32
MAX_DISTANCE = 128
Q_BLOCK = 128
NORM_EPS = 1e-6
SUBLN_EPS = 1e-5

kernel_name = "hybrid_diffattn_hyena_convffn_block"


def rms_norm(u, g, eps=NORM_EPS):
    u32 = u.astype(jnp.float32)
    y = u32 * lax.rsqrt(jnp.mean(u32 * u32, axis=-1, keepdims=True) + eps)
    return y.astype(u.dtype) * g


def dwconv3(u, w, b):
    up = jnp.pad(u, ((0, 0), (1, 1), (0, 0)))
    return up[:, :-2] * w[0] + up[:, 1:-1] * w[1] + up[:, 2:] * w[2] + b


def t5_bucket(rel):
    half = N_BUCKETS // 2
    max_exact = half // 2
    ret = (rel > 0).astype(jnp.int32) * half
    n = jnp.abs(rel)
    nf = jnp.maximum(n, 1).astype(jnp.float32)
    large = max_exact + (jnp.log(nf / max_exact) / math.log(MAX_DISTANCE / max_exact)
                         * (half - max_exact)).astype(jnp.int32)
    large = jnp.minimum(large, half - 1)
    return ret + jnp.where(n < max_exact, n, large)


def diff_attention(q, k, v, rel_bias, lam):
    b, l = q.shape[0], q.shape[1]
    n_blk = l // Q_BLOCK
    scale = ATTN_HEAD_DIM ** -0.5
    q_blocks = jnp.moveaxis(q.reshape(b, n_blk, Q_BLOCK, 2 * ATTN_HEADS, ATTN_HEAD_DIM), 1, 0)
    key_pos = jnp.arange(l, dtype=jnp.int32)

    def one_block(args):
        q_blk, blk = args
        q_pos = blk * Q_BLOCK + jnp.arange(Q_BLOCK, dtype=jnp.int32)
        bucket = t5_bucket(key_pos[None, :] - q_pos[:, None])
        bias = jnp.moveaxis(rel_bias[bucket], -1, 0).astype(jnp.float32)
        s = jnp.einsum('bqnd,bknd->bnqk', q_blk, k).astype(jnp.float32) * scale
        s = s.reshape(b, ATTN_HEADS, 2, Q_BLOCK, l) + bias[None, :, None]
        p = jax.nn.softmax(s, axis=-1)
        w = p[:, :, 0] - lam * p[:, :, 1]
        return jnp.einsum('bhqk,bkhe->bqhe', w.astype(v.dtype), v)

    out = lax.map(one_block, (q_blocks, jnp.arange(n_blk, dtype=jnp.int32)))
    return jnp.moveaxis(out, 0, 1).reshape(b, l, ATTN_HEADS, 2 * ATTN_HEAD_DIM)


def hyena_filters(l, w1, b1, w2, b2, w3, b3, w4, freq):
    f32 = jnp.float32
    t = jnp.linspace(0.0, 1.0, l, dtype=f32)[:, None]
    t_rescaled = jnp.arange(l, dtype=f32)[:, None]
    ang = 2.0 * math.pi * t_rescaled / l
    bands = jnp.linspace(1e-4, FILTER_BANDS - 1, FILTER_BANDS, dtype=f32)[None, :]
    emb = jnp.concatenate([t, jnp.cos(bands * ang), -jnp.sin(bands * ang)], axis=-1)
    fr = freq.astype(f32)
    h = jnp.sin(fr * (emb @ w1.astype(f32) + b1.astype(f32)))
    h = jnp.sin(fr * (h @ w2.astype(f32) + b2.astype(f32)))
    h = jnp.sin(fr * (h @ w3.astype(f32) + b3.astype(f32)))
    h = h @ w4.astype(f32)
    max_decay = math.log(DECAY_TARGET) / FAST_DECAY_PCT
    min_decay = math.log(DECAY_TARGET) / SLOW_DECAY_PCT
    deltas = jnp.abs(jnp.linspace(min_decay, max_decay, HYENA_WIDTH, dtype=f32))
    h = h * jnp.exp(-t * jnp.tile(deltas, 2)[None, :])
    h_fwd, h_bwd = h[:, :HYENA_WIDTH], h[:, HYENA_WIDTH:]
    zero = jnp.zeros((1, HYENA_WIDTH), f32)
    return jnp.concatenate([h_fwd[:1] + h_bwd[:1], h_fwd[1:], zero, h_bwd[1:][::-1]], axis=0)


def long_conv(u, kern):
    l = u.shape[1]
    n = 2 * l
    u_f = jnp.fft.rfft(u.astype(jnp.float32), n=n, axis=1)
    k_f = jnp.fft.rfft(kern, n=n, axis=0)
    y = jnp.fft.irfft(u_f * k_f[None], n=n, axis=1)[:, :l]
    return y.astype(u.dtype)


def setup_inputs(seed: int = 0) -> dict:
    key = jax.random.key(seed)
    ks = jax.random.split(key, 32)

    def nrm(k, shape, scale):
        return jax.random.normal(k, shape, jnp.float32) * scale

    def gain(k, shape):
        return 1.0 + nrm(k, shape, 0.01)

    nl = DEPTH
    return {
        'x': nrm(ks[0], (BATCH, SEQ, D_MODEL), 1.0),
        'g_mix': gain(ks[1], (nl, D_MODEL)),
        'w_in': nrm(ks[2], (nl, D_MODEL, IN_COLS), D_MODEL ** -0.5),
        'lambda_q1': nrm(ks[3], (nl, ATTN_HEAD_DIM), 0.1),
        'lambda_k1': nrm(ks[4], (nl, ATTN_HEAD_DIM), 0.1),
        'lambda_q2': nrm(ks[5], (nl, ATTN_HEAD_DIM), 0.1),
        'lambda_k2': nrm(ks[6], (nl, ATTN_HEAD_DIM), 0.1),
        'g_subln': gain(ks[7], (nl, 2 * ATTN_HEAD_DIM)),
        'rel_bias': nrm(ks[8], (N_BUCKETS, ATTN_HEADS), 0.5),
        'hy_conv_w': nrm(ks[9], (nl, CONV_WIDTH, HY_COLS), CONV_WIDTH ** -0.5),
        'hy_conv_b': nrm(ks[10], (nl, HY_COLS), 0.01),
        'hy_f_w1': nrm(ks[11], (nl, FILTER_EMB_DIM, FILTER_ORDER), FILTER_EMB_DIM ** -0.5),
        'hy_f_b1': nrm(ks[12], (nl, FILTER_ORDER), 0.01),
        'hy_f_w2': nrm(ks[13], (nl, FILTER_ORDER, FILTER_ORDER), FILTER_ORDER ** -0.5),
        'hy_f_b2': nrm(ks[14], (nl, FILTER_ORDER), 0.01),
        'hy_f_w3': nrm(ks[15], (nl, FILTER_ORDER, FILTER_ORDER), FILTER_ORDER ** -0.5),
        'hy_f_b3': nrm(ks[16], (nl, FILTER_ORDER), 0.01),
        'hy_f_w4': nrm(ks[17], (nl, FILTER_ORDER, 2 * HYENA_WIDTH), 0.02),
        'hy_freq': 1.0 + nrm(ks[18], (nl, FILTER_ORDER), 0.01),
        'hy_d': nrm(ks[19], (nl, HYENA_WIDTH), 1.0),
        'w_attn_branch': nrm(ks[20], (nl, V_COLS, D_MODEL), V_COLS ** -0.5),
        'w_hyena_branch': nrm(ks[21], (nl, HYENA_WIDTH, D_MODEL), HYENA_WIDTH ** -0.5),
        'w_out': nrm(ks[22], (nl, D_MODEL, D_MODEL), D_MODEL ** -0.5),
        'g_ffn': gain(ks[23], (nl, D_MODEL)),
        'w_up': nrm(ks[24], (nl, D_MODEL, 2 * D_FF), D_MODEL ** -0.5),
        'ffn_conv_w': nrm(ks[25], (nl, CONV_WIDTH, 2 * D_FF), CONV_WIDTH ** -0.5),
        'ffn_conv_b': nrm(ks[26], (nl, 2 * D_FF), 0.01),
        'w_down': nrm(ks[27], (nl, D_FF, D_MODEL), D_FF ** -0.5),
        'g_final': gain(ks[28], (D_MODEL,)),
    }


def reference(x, g_mix, w_in, lambda_q1, lambda_k1, lambda_q2, lambda_k2, g_subln, rel_bias,
              hy_conv_w, hy_conv_b, hy_f_w1, hy_f_b1, hy_f_w2, hy_f_b2, hy_f_w3, hy_f_b3,
              hy_f_w4, hy_freq, hy_d, w_attn_branch, w_hyena_branch, w_out, g_ffn, w_up,
              ffn_conv_w, ffn_conv_b, w_down, g_final):
    f32 = jnp.float32
    b, l, _ = x.shape
    splits = [Q_COLS, Q_COLS + K_COLS, Q_COLS + K_COLS + V_COLS,
              Q_COLS + K_COLS + V_COLS + HY_COLS]
    for layer in range(DEPTH):
        lambda_init = 0.8 - 0.6 * math.exp(-0.3 * layer)
        h = rms_norm(x, g_mix[layer])
        proj = jnp.einsum('bld,de->ble', h, w_in[layer])
        q, k, v, hy, gates = jnp.split(proj, splits, axis=-1)

        lam = (jnp.exp(jnp.sum(lambda_q1[layer].astype(f32) * lambda_k1[layer].astype(f32)))
               - jnp.exp(jnp.sum(lambda_q2[layer].astype(f32) * lambda_k2[layer].astype(f32)))
               + lambda_init)
        att = diff_attention(q.reshape(b, l, 2 * ATTN_HEADS, ATTN_HEAD_DIM),
                             k.reshape(b, l, 2 * ATTN_HEADS, ATTN_HEAD_DIM),
                             v.reshape(b, l, ATTN_HEADS, 2 * ATTN_HEAD_DIM),
                             rel_bias, lam)
        att = rms_norm(att, g_subln[layer], SUBLN_EPS) * (1.0 - lambda_init)
        a_branch = jnp.einsum('ble,ed->bld', att.reshape(b, l, V_COLS), w_attn_branch[layer])

        hy = dwconv3(hy, hy_conv_w[layer], hy_conv_b[layer])
        x0, x1, hv = jnp.split(hy, 3, axis=-1)
        kern = hyena_filters(l, hy_f_w1[layer], hy_f_b1[layer], hy_f_w2[layer], hy_f_b2[layer],
                             hy_f_w3[layer], hy_f_b3[layer], hy_f_w4[layer], hy_freq[layer])
        u = x1 * hv
        y_hy = x0 * (long_conv(u, kern) + u * hy_d[layer])
        h_branch = jnp.einsum('blc,cd->bld', y_hy, w_hyena_branch[layer])

        g_a, g_h = jnp.split(gates, 2, axis=-1)
        merged = jax.nn.sigmoid(g_a) * a_branch + jax.nn.sigmoid(g_h) * h_branch
        x = x + jnp.einsum('bld,de->ble', merged, w_out[layer])

        hf = rms_norm(x, g_ffn[layer])
        up = dwconv3(jnp.einsum('bld,df->blf', hf, w_up[layer]), ffn_conv_w[layer], ffn_conv_b[layer])
        gate_u, val_u = jnp.split(up, 2, axis=-1)
        x = x + jnp.einsum('blf,fd->bld', jax.nn.silu(gate_u) * val_u, w_down[layer])
    return rms_norm(x, g_final)
```

```python
import functools
import math

import numpy as np
import jax
import jax.numpy as jnp
from jax import lax
from jax.experimental import pallas as pl
from jax.experimental.pallas import tpu as pltpu

ATTN_HEADS = 8
ATTN_HEAD_DIM = 64
HEAD_COLS = 2 * ATTN_HEAD_DIM
N_BUCKETS = 32
MAX_DISTANCE = 128
NORM_EPS = 1e-6
SUBLN_EPS = 1e-5
LAMBDA_INIT = 0.8 - 0.6 * math.exp(-0.3 * 0)
FILTER_EMB_DIM = 33
FILTER_BANDS = (FILTER_EMB_DIM - 1) // 2
FAST_DECAY_PCT = 0.3
SLOW_DECAY_PCT = 1.5
DECAY_TARGET = 1e-2

V7X_LANES = 128
V7X_BF16_SUBLANES = 16
V7X_MXU_DIM = 256
V7X_VMEM_LIMIT_BYTES = 60 * 1024 * 1024

BF16 = jnp.bfloat16
F32 = jnp.float32


def _params(*sem):
    return pltpu.CompilerParams(dimension_semantics=sem, vmem_limit_bytes=V7X_VMEM_LIMIT_BYTES)


def _dot(a, b):
    return jnp.dot(a, b, preferred_element_type=F32)


def _dot_nt(a, b):
    return lax.dot_general(a, b, (((1,), (1,)), ((), ())), preferred_element_type=F32)


def _norm_proj_kernel(x_ref, g_ref, w_ref, o_ref, xn_ref):
    @pl.when(pl.program_id(1) == 0)
    def _():
        x = x_ref[...]
        ms = jnp.mean(x * x, axis=-1, keepdims=True)
        xn_ref[...] = (x * lax.rsqrt(ms + NORM_EPS) * g_ref[...]).astype(BF16)

    o_ref[...] = _dot(xn_ref[...], w_ref[...]).astype(o_ref.dtype)


def _norm_proj(x2, g, w, col0, ncols, out_dtype, tm, tn):
    m, d = x2.shape
    assert col0 % tn == 0 and ncols % tn == 0 and m % tm == 0
    off = col0 // tn
    return pl.pallas_call(
        _norm_proj_kernel,
        grid=(m // tm, ncols // tn),
        in_specs=[
            pl.BlockSpec((tm, d), lambda i, j: (i, 0)),
            pl.BlockSpec((1, d), lambda i, j: (0, 0)),
            pl.BlockSpec((d, tn), lambda i, j: (0, j + off)),
        ],
        out_specs=pl.BlockSpec((tm, tn), lambda i, j: (i, j)),
        out_shape=jax.ShapeDtypeStruct((m, ncols), out_dtype),
        scratch_shapes=[pltpu.VMEM((tm, d), BF16)],
        compiler_params=_params("arbitrary", "arbitrary"),
        name="norm_proj",
    )(x2, g, w)


def _t5_large_thresholds():
    half = N_BUCKETS // 2
    max_exact = half // 2
    n = np.arange(max_exact, 4 * MAX_DISTANCE, dtype=np.float64)
    large = max_exact + (np.log(n / max_exact) / math.log(MAX_DISTANCE / max_exact)
                         * (half - max_exact)).astype(np.int64)
    large = np.minimum(large, half - 1)
    thr = [int(n[np.argmax(large >= b)]) for b in range(max_exact + 1, half)]
    assert large[-1] == half - 1 and thr[-1] < MAX_DISTANCE
    return max_exact, half, thr


def _bias_tiles_kernel(rb_ref, o_ref, *, t):
    h = pl.program_id(0)
    max_exact, half, thr = _t5_large_thresholds()
    qi = lax.broadcasted_iota(jnp.int32, (t, t), 0)
    kj = lax.broadcasted_iota(jnp.int32, (t, t), 1)
    for d in range(5):
        rel = (d - 2) * t + kj - qi
        n = jnp.abs(rel)
        large = jnp.full((t, t), max_exact, jnp.int32)
        for th in thr:
            large = large + (n >= th).astype(jnp.int32)
        bucket = jnp.where(rel > 0, half, 0) + jnp.where(n < max_exact, n, large)
        tile = jnp.zeros((t, t), F32)
        for b in range(N_BUCKETS):
            tile = jnp.where(bucket == b, rb_ref[b, h], tile)
        o_ref[d] = tile


def _bias_tiles(rel_bias, t):
    assert t + 1 >= MAX_DISTANCE
    return pl.pallas_call(
        functools.partial(_bias_tiles_kernel, t=t),
        grid=(ATTN_HEADS,),
        in_specs=[pl.BlockSpec(memory_space=pltpu.SMEM)],
        out_specs=pl.BlockSpec((None, 5, t, t), lambda h: (h, 0, 0, 0)),
        out_shape=jax.ShapeDtypeStruct((ATTN_HEADS, 5, t, t), F32),
        compiler_params=_params("arbitrary"),
        name="t5_bias_tiles",
    )(rel_bias)


def _attn_kernel(q_ref, k_ref, v_ref, bias_ref, lam_ref, g_ref, o_ref, s_ref, vext_ref, *, t, nkb):
    qb = pl.program_id(2)

    @pl.when(qb == 0)
    def _():
        vext_ref[:, :HEAD_COLS] = v_ref[...]
        vext_ref[:, HEAD_COLS:] = jnp.ones((vext_ref.shape[0], HEAD_COLS), BF16)

    lq1, lk1, lq2, lk2 = (lam_ref[i:i + 1, :] for i in range(4))
    lam = (jnp.exp(jnp.sum(lq1 * lk1, keepdims=True)) - jnp.exp(jnp.sum(lq2 * lk2, keepdims=True))
           + LAMBDA_INIT)

    q = q_ref[...] * (ATTN_HEAD_DIM ** -0.5)
    lane = lax.broadcasted_iota(jnp.int32, q.shape, 1)
    qs = (jnp.where(lane < ATTN_HEAD_DIM, q, 0).astype(BF16),
          jnp.where(lane >= ATTN_HEAD_DIM, q, 0).astype(BF16))

    m = [jnp.full((t, 1), -jnp.inf, F32) for _ in range(2)]
    for kb in range(nkb):
        kblk = k_ref[kb * t:(kb + 1) * t, :]
        bias = bias_ref[jnp.clip(kb - qb, -2, 2) + 2]
        for j in range(2):
            s = _dot_nt(qs[j], kblk) + bias
            s_ref[j, :, kb * t:(kb + 1) * t] = s
            m[j] = jnp.maximum(m[j], jnp.max(s, axis=-1, keepdims=True))

    outs = []
    for j in range(2):
        acc = jnp.zeros((t, 2 * HEAD_COLS), F32)
        for kb in range(nkb):
            p = jnp.exp(s_ref[j, :, kb * t:(kb + 1) * t] - m[j]).astype(BF16)
            acc = acc + _dot(p, vext_ref[kb * t:(kb + 1) * t, :])
        outs.append(acc[:, :HEAD_COLS] / acc[:, HEAD_COLS:HEAD_COLS + 1])

    att = outs[0] - lam * outs[1]
    ms = jnp.mean(att * att, axis=-1, keepdims=True)
    att = att * lax.rsqrt(ms + SUBLN_EPS) * g_ref[...] * (1.0 - LAMBDA_INIT)
    o_ref[...] = att.astype(o_ref.dtype)


def _attention(qkv3, bias_tiles, lam_vecs, g_subln, t):
    b, l, _ = qkv3.shape
    nkb = l // t
    kcol0 = ATTN_HEADS
    vcol0 = 2 * ATTN_HEADS
    return pl.pallas_call(
        functools.partial(_attn_kernel, t=t, nkb=nkb),
        grid=(b, ATTN_HEADS, l // t),
        in_specs=[
            pl.BlockSpec((None, t, HEAD_COLS), lambda bi, h, i: (bi, i, h)),
            pl.BlockSpec((None, l, HEAD_COLS), lambda bi, h, i: (bi, 0, kcol0 + h)),
            pl.BlockSpec((None, l, HEAD_COLS), lambda bi, h, i: (bi, 0, vcol0 + h)),
            pl.BlockSpec((None, 5, t, t), lambda bi, h, i: (h, 0, 0, 0)),
            pl.BlockSpec((4, ATTN_HEAD_DIM), lambda bi, h, i: (0, 0)),
            pl.BlockSpec((1, HEAD_COLS), lambda bi, h, i: (0, 0)),
        ],
        out_specs=pl.BlockSpec((None, t, HEAD_COLS), lambda bi, h, i: (bi, i, h)),
        out_shape=jax.ShapeDtypeStruct((b, l, ATTN_HEADS * HEAD_COLS), BF16),
        scratch_shapes=[pltpu.VMEM((2, t, l), F32), pltpu.VMEM((l, 2 * HEAD_COLS), BF16)],
        compiler_params=_params("arbitrary", "arbitrary", "arbitrary"),
        name="diff_attention",
    )(qkv3, qkv3, qkv3, bias_tiles, lam_vecs, g_subln)


def _filter_kernel(w1t_ref, w1c_ref, w1s_ref, b1_ref, w2_ref, b2_ref, w3_ref, b3_ref, w4_ref, fr_ref, o_ref,
                   *, l, tl, width):
    i = pl.program_id(0)
    pos = (lax.broadcasted_iota(jnp.int32, (tl, 1), 0) + i * tl).astype(F32)
    t = pos / (l - 1)
    ang = 2.0 * math.pi * pos / l
    bi = lax.broadcasted_iota(jnp.int32, (1, FILTER_BANDS), 1).astype(F32)
    bands = 1e-4 + bi * ((FILTER_BANDS - 1 - 1e-4) / (FILTER_BANDS - 1))
    pre = t * w1t_ref[...] + _dot(jnp.cos(bands * ang), w1c_ref[...]) - _dot(jnp.sin(bands * ang), w1s_ref[...])
    fr = fr_ref[...]
    h = jnp.sin(fr * (pre + b1_ref[...]))
    h = jnp.sin(fr * (_dot(h, w2_ref[...]) + b2_ref[...]))
    h = jnp.sin(fr * (_dot(h, w3_ref[...]) + b3_ref[...]))
    h = _dot(h, w4_ref[...])
    max_decay = math.log(DECAY_TARGET) / FAST_DECAY_PCT
    min_decay = math.log(DECAY_TARGET) / SLOW_DECAY_PCT
    ci = lax.broadcasted_iota(jnp.int32, (1, 2 * width), 1)
    ci = jnp.where(ci >= width, ci - width, ci).astype(F32)
    deltas = jnp.abs(min_decay + ci * ((max_decay - min_decay) / (width - 1)))
    o_ref[...] = h * jnp.exp(-t * deltas)


def _hyena_filter(l, w1, b1, w2, b2, w3, b3, w4, freq, tl):
    width = w4.shape[1] // 2
    full = lambda a: pl.BlockSpec(a.shape, lambda i: (0,) * a.ndim)
    args = (w1[0:1], w1[1:1 + FILTER_BANDS], w1[1 + FILTER_BANDS:], b1, w2, b2, w3, b3, w4, freq)
    return pl.pallas_call(
        functools.partial(_filter_kernel, l=l, tl=tl, width=width),
        grid=(l // tl,),
        in_specs=[full(a) for a in args],
        out_specs=pl.BlockSpec((tl, 2 * width), lambda i: (i, 0)),
        out_shape=jax.ShapeDtypeStruct((l, 2 * width), F32),
        compiler_params=_params("arbitrary"),
        name="hyena_filter",
    )(*args)


def _shifted_rows(z):
    n = z.shape[0]
    row = lax.broadcasted_iota(jnp.int32, z.shape, 0)
    prev = jnp.where(row == 0, 0.0, pltpu.roll(z, 1, 0))
    nxt = jnp.where(row == n - 1, 0.0, pltpu.roll(z, n - 1, 0))
    return prev, nxt


def _dwconv_rows(z, w, b):
    prev, nxt = _shifted_rows(z)
    return prev * w[0:1, :] + z * w[1:2, :] + nxt * w[2:3, :] + b


def _hy_gate_kernel(a0, a1, a2, w0, w1, w2, b0, b1, b2, x0_ref, u_ref):
    x0_ref[...] = _dwconv_rows(a0[...], w0[...], b0[...])
    u_ref[...] = _dwconv_rows(a1[...], w1[...], b1[...]) * _dwconv_rows(a2[...], w2[...], b2[...])


def _hy_gate(hy3, conv_w, conv_b, ct):
    b, l, cols = hy3.shape
    c = cols // 3
    nct = c // ct
    a_spec = lambda k: pl.BlockSpec((None, l, ct), lambda bi, ci: (bi, 0, k * nct + ci))
    w_spec = lambda k: pl.BlockSpec((3, ct), lambda bi, ci: (0, k * nct + ci))
    b_spec = lambda k: pl.BlockSpec((1, ct), lambda bi, ci: (0, k * nct + ci))
    o_spec = pl.BlockSpec((None, l, ct), lambda bi, ci: (bi, 0, ci))
    return pl.pallas_call(
        _hy_gate_kernel,
        grid=(b, nct),
        in_specs=[a_spec(0), a_spec(1), a_spec(2), w_spec(0), w_spec(1), w_spec(2),
                  b_spec(0), b_spec(1), b_spec(2)],
        out_specs=[o_spec, o_spec],
        out_shape=[jax.ShapeDtypeStruct((b, l, c), F32)] * 2,
        compiler_params=_params("arbitrary", "arbitrary"),
        name="hyena_gate",
    )(hy3, hy3, hy3, conv_w, conv_w, conv_w, conv_b, conv_b, conv_b)


FFT_N2 = V7X_MXU_DIM
FFT_R = V7X_BF16_SUBLANES


@functools.lru_cache(maxsize=None)
def _fft_constants(l):
    n = 2 * l
    n2 = FFT_N2
    n1 = n // n2
    assert n1 * n2 == n and n1 % 2 == 0
    r = FFT_R
    eye = np.eye(r)
    k1 = np.arange(n1)[:, None]
    a1 = 2.0 * np.pi * k1 * np.arange(n1 // 2)[None, :] / n1
    m1r = np.kron(np.cos(a1), eye)
    m1i = np.kron(-np.sin(a1), eye)
    m1ir = np.kron(np.cos(a1).T, eye) / n
    m1ii = np.kron(np.sin(a1).T, eye) / n
    a2 = 2.0 * np.pi * np.arange(n2)[:, None] * np.arange(n2)[None, :] / n2
    f2r, f2i = np.cos(a2), -np.sin(a2)
    at = 2.0 * np.pi * k1 * np.arange(n2)[None, :] / n
    twr = np.broadcast_to(np.cos(at)[:, :, None], (n1, n2, V7X_LANES))
    twi = np.broadcast_to(-np.sin(at)[:, :, None], (n1, n2, V7X_LANES))
    f = lambda a: np.ascontiguousarray(a, dtype=np.float32)
    return dict(n1=n1, n2=n2, mats=tuple(f(a) for a in (m1r, m1i, m1ir, m1ii, f2r, f2i)),
                tw=(f(twr), f(twi)))


def _hyena_conv_kernel(u_ref, x0_ref, hf_ref, hb_ref, d_ref, m1r_ref, m1i_ref, m1ir_ref, m1ii_ref,
                       f2r_ref, f2i_ref, twr_ref, twi_ref, o_ref, a_ref, k_ref, *, n1, n2, ct):
    r = FFT_R
    h1 = n1 // 2
    nblk = n2 // r
    reps = ct // V7X_LANES

    def rows(ref_slice_fn, count):
        return jnp.concatenate([ref_slice_fn(i) for i in range(count)], axis=0)

    def twiddle(tr, ti):
        if reps == 1:
            return tr, ti
        return jnp.concatenate([tr] * reps, axis=-1), jnp.concatenate([ti] * reps, axis=-1)

    def blk(j):
        return pl.ds(pl.multiple_of(j * r, r), r)

    def strided_fwd(gr, gi, j):
        m1r, m1i = m1r_ref[...], m1i_ref[...]
        ar, ai = _dot(m1r, gr), _dot(m1i, gr)
        if gi is not None:
            ar, ai = ar - _dot(m1i, gi), ai + _dot(m1r, gi)
        tr, ti = twiddle(rows(lambda i: twr_ref[i, blk(j), :], n1), rows(lambda i: twi_ref[i, blk(j), :], n1))
        br = (ar * tr - ai * ti).astype(BF16)
        bi = (ar * ti + ai * tr).astype(BF16)
        for i in range(n1):
            a_ref[0, i, blk(j), :] = br[i * r:(i + 1) * r]
            a_ref[1, i, blk(j), :] = bi[i * r:(i + 1) * r]

    for part in range(2):
        def f_stage1(j, _):
            hf = rows(lambda i: hf_ref[i, blk(j), :], h1)
            hb = rows(lambda i: hb_ref[i, blk(j), :], h1)
            g = (hf + hb) if part == 0 else (hf - hb)
            strided_fwd(g.astype(BF16), None, j)
            return 0
        lax.fori_loop(0, nblk, f_stage1, 0)

        def f_stage2(i, _):
            ar, ai = a_ref[0, i], a_ref[1, i]
            if part == 0:
                k_ref[0, i] = _dot(f2r_ref[...], ar) - _dot(f2i_ref[...], ai)
            else:
                k_ref[1, i] = _dot(f2r_ref[...], ai) + _dot(f2i_ref[...], ar)
            return 0
        lax.fori_loop(0, n1, f_stage2, 0)

    def u_stage1(j, _):
        gr = rows(lambda i: u_ref[0, i, blk(j), :], h1).astype(BF16)
        gi = rows(lambda i: u_ref[1, i, blk(j), :], h1).astype(BF16)
        strided_fwd(gr, gi, j)
        return 0
    lax.fori_loop(0, nblk, u_stage1, 0)

    def mid(i, _):
        f2r, f2i = f2r_ref[...], f2i_ref[...]
        ar, ai = a_ref[0, i], a_ref[1, i]
        xr = _dot(f2r, ar) - _dot(f2i, ai)
        xi = _dot(f2r, ai) + _dot(f2i, ar)
        kr, ki = k_ref[0, i], k_ref[1, i]
        yr = (xr * kr - xi * ki).astype(BF16)
        yi = (xr * ki + xi * kr).astype(BF16)
        br = _dot(f2r, yr) + _dot(f2i, yi)
        bi = _dot(f2r, yi) - _dot(f2i, yr)
        tr, ti = twiddle(twr_ref[i], twi_ref[i])
        a_ref[0, i] = (br * tr + bi * ti).astype(BF16)
        a_ref[1, i] = (bi * tr - br * ti).astype(BF16)
        return 0
    lax.fori_loop(0, n1, mid, 0)

    def last(j, _):
        br = rows(lambda i: a_ref[0, i, blk(j), :], n1)
        bi = rows(lambda i: a_ref[1, i, blk(j), :], n1)
        m1ir, m1ii = m1ir_ref[...], m1ii_ref[...]
        y = (_dot(m1ir, br) - _dot(m1ii, bi), _dot(m1ir, bi) + _dot(m1ii, br))
        d = d_ref[...]
        for bidx in range(2):
            for i in range(h1):
                u = u_ref[bidx, i, blk(j), :]
                x0 = x0_ref[bidx, i, blk(j), :]
                o_ref[bidx, i, blk(j), :] = (x0 * (y[bidx][i * r:(i + 1) * r] + u * d)).astype(o_ref.dtype)
        return 0
    lax.fori_loop(0, nblk, last, 0)


def _hyena_conv(u, x0, filt, hy_d, ct):
    b, l, c = u.shape
    assert b == 2, "the two batch rows are packed as real / imaginary parts of one complex transform"
    consts = _fft_constants(l)
    n1, n2 = consts["n1"], consts["n2"]
    h1 = n1 // 2
    nct = c // ct
    u4 = u.reshape(b, h1, n2, c)
    x04 = x0.reshape(b, h1, n2, c)
    f3 = filt.reshape(h1, n2, 2 * c)
    mats = [jnp.asarray(a, dtype=BF16) for a in consts["mats"]]
    tws = [jnp.asarray(a) for a in consts["tw"]]
    const_spec = lambda a: pl.BlockSpec(a.shape, lambda ci: (0,) * a.ndim, pipeline_mode=pl.Buffered(1))
    out = pl.pallas_call(
        functools.partial(_hyena_conv_kernel, n1=n1, n2=n2, ct=ct),
        grid=(nct,),
        in_specs=[
            pl.BlockSpec((b, h1, n2, ct), lambda ci: (0, 0, 0, ci)),
            pl.BlockSpec((b, h1, n2, ct), lambda ci: (0, 0, 0, ci)),
            pl.BlockSpec((h1, n2, ct), lambda ci: (0, 0, ci)),
            pl.BlockSpec((h1, n2, ct), lambda ci: (0, 0, nct + ci)),
            pl.BlockSpec((1, ct), lambda ci: (0, ci)),
        ] + [const_spec(a) for a in mats] + [const_spec(a) for a in tws],
        out_specs=pl.BlockSpec((b, h1, n2, ct), lambda ci: (0, 0, 0, ci)),
        out_shape=jax.ShapeDtypeStruct((b, h1, n2, c), BF16),
        scratch_shapes=[pltpu.VMEM((2, n1, n2, ct), BF16), pltpu.VMEM((2, n1, n2, ct), F32)],
        compiler_params=_params("arbitrary"),
        name="hyena_long_conv",
    )(u4, x04, f3, f3, hy_d, *mats, *tws)
    return out.reshape(b * l, c)


def _merge_kernel(att_ref, yhy_ref, ga_ref, gh_ref, x_ref, wa_ref, wh_ref, wo_ref, g_ref, x1_ref, hf_ref):
    a = _dot(att_ref[...], wa_ref[...])
    h = _dot(yhy_ref[...], wh_ref[...])
    merged = jax.nn.sigmoid(ga_ref[...]) * a + jax.nn.sigmoid(gh_ref[...]) * h
    x1 = x_ref[...] + _dot(merged.astype(BF16), wo_ref[...])
    x1_ref[...] = x1
    ms = jnp.mean(x1 * x1, axis=-1, keepdims=True)
    hf_ref[...] = (x1 * lax.rsqrt(ms + NORM_EPS) * g_ref[...]).astype(BF16)


def _merge(att, yhy, gates, x2, wa, wh, wo, g_ffn, tm):
    m, d = x2.shape
    row = lambda w: pl.BlockSpec((tm, w), lambda i: (i, 0))
    const = lambda a: pl.BlockSpec(a.shape, lambda i: (0, 0), pipeline_mode=pl.Buffered(1))
    return pl.pallas_call(
        _merge_kernel,
        grid=(m // tm,),
        in_specs=[row(att.shape[1]), row(yhy.shape[1]),
                  pl.BlockSpec((tm, d), lambda i: (i, 0)), pl.BlockSpec((tm, d), lambda i: (i, 1)),
                  row(d), const(wa), const(wh), const(wo), const(g_ffn)],
        out_specs=[row(d), row(d)],
        out_shape=[jax.ShapeDtypeStruct((m, d), F32), jax.ShapeDtypeStruct((m, d), BF16)],
        compiler_params=_params("arbitrary"),
        name="merge_out_proj",
    )(att, yhy, gates, gates, x2, wa, wh, wo, g_ffn)


FFN_HALO = V7X_BF16_SUBLANES


def _ffn_kernel(hf_ref, prev_ref, next_ref, x1_ref, wg_ref, wv_ref, cwg_ref, cwv_ref, cbg_ref, cbv_ref,
                wd_ref, g_ref, o_ref, ext_ref, acc_ref, *, tm, tiles_per_seq):
    i = pl.program_id(0)
    f = pl.program_id(1)
    hl = FFN_HALO

    @pl.when(f == 0)
    def _():
        first = (i % tiles_per_seq) == 0
        last = (i % tiles_per_seq) == tiles_per_seq - 1
        ext_ref[0:hl, :] = jnp.where(first, jnp.zeros_like(prev_ref[...]), prev_ref[...])
        ext_ref[hl:hl + tm, :] = hf_ref[...]
        ext_ref[hl + tm:, :] = jnp.where(last, jnp.zeros_like(next_ref[...]), next_ref[...])
        acc_ref[...] = jnp.zeros_like(acc_ref)

    ext = ext_ref[...]
    ne = tm + 2 * hl

    def conv(w_ref, cw_ref, cb_ref):
        up = _dot(ext, w_ref[...])
        cw = cw_ref[...]
        prev = pltpu.roll(up, 1, 0)
        nxt = pltpu.roll(up, ne - 1, 0)
        full = prev * cw[0:1, :] + up * cw[1:2, :] + nxt * cw[2:3, :] + cb_ref[...]
        return full[hl:hl + tm]

    gate = conv(wg_ref, cwg_ref, cbg_ref)
    val = conv(wv_ref, cwv_ref, cbv_ref)
    act = (gate * jax.nn.sigmoid(gate) * val).astype(BF16)
    acc_ref[...] += _dot(act, wd_ref[...])

    @pl.when(f == pl.num_programs(1) - 1)
    def _():
        x2 = x1_ref[...] + acc_ref[...]
        ms = jnp.mean(x2 * x2, axis=-1, keepdims=True)
        o_ref[...] = x2 * lax.rsqrt(ms + NORM_EPS) * g_ref[...]


def _ffn(hf, x1, w_up, conv_w, conv_b, w_down, g_final, l, tm, tf):
    m, d = hf.shape
    dff = w_down.shape[0]
    nf = dff // tf
    hl = FFN_HALO
    assert l % tm == 0 and tm % hl == 0 and dff % tf == 0
    hb = tm // hl
    nhb = m // hl
    return pl.pallas_call(
        functools.partial(_ffn_kernel, tm=tm, tiles_per_seq=l // tm),
        grid=(m // tm, nf),
        in_specs=[
            pl.BlockSpec((tm, d), lambda i, f: (i, 0)),
            pl.BlockSpec((hl, d), lambda i, f: (jnp.maximum(i * hb - 1, 0), 0)),
            pl.BlockSpec((hl, d), lambda i, f: (jnp.minimum((i + 1) * hb, nhb - 1), 0)),
            pl.BlockSpec((tm, d), lambda i, f: (i, 0)),
            pl.BlockSpec((d, tf), lambda i, f: (0, f)),
            pl.BlockSpec((d, tf), lambda i, f: (0, nf + f)),
            pl.BlockSpec((3, tf), lambda i, f: (0, f)),
            pl.BlockSpec((3, tf), lambda i, f: (0, nf + f)),
            pl.BlockSpec((1, tf), lambda i, f: (0, f)),
            pl.BlockSpec((1, tf), lambda i, f: (0, nf + f)),
            pl.BlockSpec((tf, d), lambda i, f: (f, 0)),
            pl.BlockSpec((1, d), lambda i, f: (0, 0)),
        ],
        out_specs=pl.BlockSpec((tm, d), lambda i, f: (i, 0)),
        out_shape=jax.ShapeDtypeStruct((m, d), F32),
        scratch_shapes=[pltpu.VMEM((tm + 2 * hl, d), BF16), pltpu.VMEM((tm, d), F32)],
        compiler_params=_params("arbitrary", "arbitrary"),
        name="conv_ffn",
    )(hf, hf, hf, x1, w_up, w_up, conv_w, conv_w, conv_b, conv_b, w_down, g_final)


def _tile(n, want):
    t = min(n, want)
    assert n % t == 0
    return t


def kernel(x, g_mix, w_in, lambda_q1, lambda_k1, lambda_q2, lambda_k2, g_subln, rel_bias, hy_conv_w,
           hy_conv_b, hy_f_w1, hy_f_b1, hy_f_w2, hy_f_b2, hy_f_w3, hy_f_b3, hy_f_w4, hy_freq, hy_d,
           w_attn_branch, w_hyena_branch, w_out, g_ffn, w_up, ffn_conv_w, ffn_conv_b, w_down, g_final):
    b, l, d = x.shape
    m = b * l
    depth = w_in.shape[0]
    assert depth == 1
    qkv_cols = 3 * ATTN_HEADS * HEAD_COLS
    width = hy_d.shape[1]
    hy_cols = 3 * width
    gate_cols = 2 * d
    assert w_in.shape[2] == qkv_cols + hy_cols + gate_cols

    x2 = x.reshape(m, d)
    row = lambda v: v.reshape(1, -1)

    w_in_b = w_in[0].astype(BF16)
    tm_proj = _tile(m, 1024)
    qkv = _norm_proj(x2, g_mix, w_in_b, 0, qkv_cols, BF16, tm_proj, _tile(qkv_cols, 512))
    hyp = _norm_proj(x2, g_mix, w_in_b, qkv_cols, hy_cols, F32, tm_proj, _tile(hy_cols, 512))
    gates = _norm_proj(x2, g_mix, w_in_b, qkv_cols + hy_cols, gate_cols, F32, tm_proj, _tile(gate_cols, 512))

    t_attn = _tile(l, 256)
    bias_tiles = _bias_tiles(rel_bias, t_attn)
    lam_vecs = jnp.concatenate([lambda_q1, lambda_k1, lambda_q2, lambda_k2], axis=0)
    att = _attention(qkv.reshape(b, l, qkv_cols), bias_tiles, lam_vecs, g_subln, t_attn)
    att = att.reshape(m, ATTN_HEADS * HEAD_COLS)

    filt = _hyena_filter(l, hy_f_w1[0], hy_f_b1, hy_f_w2[0], hy_f_b2, hy_f_w3[0], hy_f_b3, hy_f_w4[0],
                         hy_freq, _tile(l, 512))
    x0, u = _hy_gate(hyp.reshape(b, l, hy_cols), hy_conv_w[0], hy_conv_b, _tile(width, V7X_LANES))
    yhy = _hyena_conv(u, x0, filt, hy_d, _tile(width, V7X_LANES))

    x1, hf = _merge(att, yhy, gates, x2, w_attn_branch[0].astype(BF16), w_hyena_branch[0].astype(BF16),
                    w_out[0].astype(BF16), g_ffn, _tile(m, 256))

    out = _ffn(hf, x1, w_up[0].astype(BF16), ffn_conv_w[0], ffn_conv_b, w_down[0].astype(BF16),
               row(g_final), l, _tile(l, 512), _tile(w_down.shape[1], 512))
    return out.reshape(b, l, d)
```

```python
import functools
import math

import numpy as np
import jax
import jax.numpy as jnp
from jax import lax
from jax.experimental import pallas as pl
from jax.experimental.pallas import tpu as pltpu

ATTN_HEADS = 8
ATTN_HEAD_DIM = 64
HEAD_COLS = 2 * ATTN_HEAD_DIM
N_BUCKETS = 32
MAX_DISTANCE = 128
NORM_EPS = 1e-6
SUBLN_EPS = 1e-5
LAMBDA_INIT = 0.8 - 0.6 * math.exp(-0.3 * 0)
FILTER_EMB_DIM = 33
FILTER_BANDS = (FILTER_EMB_DIM - 1) // 2
FAST_DECAY_PCT = 0.3
SLOW_DECAY_PCT = 1.5
DECAY_TARGET = 1e-2

V7X_LANES = 128
V7X_BF16_SUBLANES = 16
V7X_MXU_DIM = 256
V7X_VMEM_LIMIT_BYTES = 60 * 1024 * 1024

BF16 = jnp.bfloat16
F32 = jnp.float32


def _params(*sem):
    return pltpu.CompilerParams(dimension_semantics=sem, vmem_limit_bytes=V7X_VMEM_LIMIT_BYTES)


def _dot(a, b):
    return jnp.dot(a, b, preferred_element_type=F32)


def _dot_nt(a, b):
    return lax.dot_general(a, b, (((1,), (1,)), ((), ())), preferred_element_type=F32)


def _norm_proj_kernel(x_ref, g_ref, w_ref, o_ref, xn_ref):
    @pl.when(pl.program_id(1) == 0)
    def _():
        x = x_ref[...]
        ms = jnp.mean(x * x, axis=-1, keepdims=True)
        xn_ref[...] = (x * lax.rsqrt(ms + NORM_EPS) * g_ref[...]).astype(BF16)

    o_ref[...] = _dot(xn_ref[...], w_ref[...]).astype(o_ref.dtype)


def _norm_proj(x2, g, w, col0, ncols, out_dtype, tm, tn):
    m, d = x2.shape
    assert col0 % tn == 0 and ncols % tn == 0 and m % tm == 0
    off = col0 // tn
    return pl.pallas_call(
        _norm_proj_kernel,
        grid=(m // tm, ncols // tn),
        in_specs=[
            pl.BlockSpec((tm, d), lambda i, j: (i, 0)),
            pl.BlockSpec((1, d), lambda i, j: (0, 0)),
            pl.BlockSpec((d, tn), lambda i, j: (0, j + off)),
        ],
        out_specs=pl.BlockSpec((tm, tn), lambda i, j: (i, j)),
        out_shape=jax.ShapeDtypeStruct((m, ncols), out_dtype),
        scratch_shapes=[pltpu.VMEM((tm, d), BF16)],
        compiler_params=_params("arbitrary", "arbitrary"),
        name="norm_proj",
    )(x2, g, w)


def _t5_large_thresholds():
    half = N_BUCKETS // 2
    max_exact = half // 2
    n = np.arange(max_exact, 4 * MAX_DISTANCE, dtype=np.float64)
    large = max_exact + (np.log(n / max_exact) / math.log(MAX_DISTANCE / max_exact)
                         * (half - max_exact)).astype(np.int64)
    large = np.minimum(large, half - 1)
    thr = [int(n[np.argmax(large >= b)]) for b in range(max_exact + 1, half)]
    assert large[-1] == half - 1 and thr[-1] < MAX_DISTANCE
    return max_exact, half, thr


def _bias_tiles_kernel(rb_ref, o_ref, *, t):
    h = pl.program_id(0)
    max_exact, half, thr = _t5_large_thresholds()
    qi = lax.broadcasted_iota(jnp.int32, (t, t), 0)
    kj = lax.broadcasted_iota(jnp.int32, (t, t), 1)
    for d in range(5):
        rel = (d - 2) * t + kj - qi
        n = jnp.abs(rel)
        large = jnp.full((t, t), max_exact, jnp.int32)
        for th in thr:
            large = large + (n >= th).astype(jnp.int32)
        bucket = jnp.where(rel > 0, half, 0) + jnp.where(n < max_exact, n, large)
        tile = jnp.zeros((t, t), F32)
        for b in range(N_BUCKETS):
            tile = jnp.where(bucket == b, rb_ref[b, h], tile)
        o_ref[d] = tile


def _bias_tiles(rel_bias, t):
    assert t + 1 >= MAX_DISTANCE
    return pl.pallas_call(
        functools.partial(_bias_tiles_kernel, t=t),
        grid=(ATTN_HEADS,),
        in_specs=[pl.BlockSpec(memory_space=pltpu.SMEM)],
        out_specs=pl.BlockSpec((None, 5, t, t), lambda h: (h, 0, 0, 0)),
        out_shape=jax.ShapeDtypeStruct((ATTN_HEADS, 5, t, t), F32),
        compiler_params=_params("arbitrary"),
        name="t5_bias_tiles",
    )(rel_bias)


def _attn_kernel(q_ref, k_ref, v_ref, bias_ref, lam_ref, g_ref, o_ref,
                 sa_ref, sb_ref, ma_ref, mb_ref, vext_ref, *, t, nkb, nq, n_items):
    step = pl.program_id(0)
    qb = jnp.minimum(step, n_items - 1) % nq
    done_item = jnp.maximum(step - 1, 0)

    @pl.when(step == 0)
    def _():
        sb_ref[...] = jnp.zeros_like(sb_ref)
        mb_ref[...] = jnp.zeros_like(mb_ref)

    @pl.when(done_item % nq == 0)
    def _():
        vext_ref[:, :HEAD_COLS] = v_ref[...]
        vext_ref[:, HEAD_COLS:] = jnp.ones((vext_ref.shape[0], HEAD_COLS), BF16)

    def body(cur_s, cur_m, prev_s, prev_m):
        q = q_ref[...] * (ATTN_HEAD_DIM ** -0.5)
        lane = lax.broadcasted_iota(jnp.int32, q.shape, 1)
        qs = (jnp.where(lane < ATTN_HEAD_DIM, q, 0).astype(BF16),
              jnp.where(lane >= ATTN_HEAD_DIM, q, 0).astype(BF16))
        m = [jnp.full((t, 1), -jnp.inf, F32) for _ in range(2)]
        for kb in range(nkb):
            kblk = k_ref[kb * t:(kb + 1) * t, :]
            bias = bias_ref[jnp.clip(kb - qb, -2, 2) + 2]
            for j in range(2):
                s = _dot_nt(qs[j], kblk) + bias
                cur_s[j, :, kb * t:(kb + 1) * t] = s
                m[j] = jnp.maximum(m[j], jnp.max(s, axis=-1, keepdims=True))
        for j in range(2):
            cur_m[j] = m[j]

        lq1, lk1, lq2, lk2 = (lam_ref[i:i + 1, :] for i in range(4))
        lam = (jnp.exp(jnp.sum(lq1 * lk1, keepdims=True)) - jnp.exp(jnp.sum(lq2 * lk2, keepdims=True))
               + LAMBDA_INIT)
        outs = []
        for j in range(2):
            mj = prev_m[j]
            acc = jnp.zeros((t, 2 * HEAD_COLS), F32)
            for kb in range(nkb):
                p = jnp.exp(prev_s[j, :, kb * t:(kb + 1) * t] - mj).astype(BF16)
                acc = acc + _dot(p, vext_ref[kb * t:(kb + 1) * t, :])
            outs.append(acc[:, :HEAD_COLS] / acc[:, HEAD_COLS:HEAD_COLS + 1])
        att = outs[0] - lam * outs[1]
        ms = jnp.mean(att * att, axis=-1, keepdims=True)
        att = att * lax.rsqrt(ms + SUBLN_EPS) * g_ref[...] * (1.0 - LAMBDA_INIT)
        o_ref[...] = att.astype(o_ref.dtype)

    @pl.when(step % 2 == 0)
    def _():
        body(sa_ref, ma_ref, sb_ref, mb_ref)

    @pl.when(step % 2 == 1)
    def _():
        body(sb_ref, mb_ref, sa_ref, ma_ref)


def _attention(qkv3, bias_tiles, lam_vecs, g_subln, t):
    b, l, _ = qkv3.shape
    nkb = nq = l // t
    n_items = b * ATTN_HEADS * nq
    kcol0 = ATTN_HEADS
    vcol0 = 2 * ATTN_HEADS

    def item(s):
        return s // (ATTN_HEADS * nq), (s // nq) % ATTN_HEADS, s % nq

    def scored(s):
        return item(jnp.minimum(s, n_items - 1))

    def finished(s):
        return item(jnp.maximum(s - 1, 0))

    return pl.pallas_call(
        functools.partial(_attn_kernel, t=t, nkb=nkb, nq=nq, n_items=n_items),
        grid=(n_items + 1,),
        in_specs=[
            pl.BlockSpec((None, t, HEAD_COLS), lambda s: (scored(s)[0], scored(s)[2], scored(s)[1])),
            pl.BlockSpec((None, l, HEAD_COLS), lambda s: (scored(s)[0], 0, kcol0 + scored(s)[1])),
            pl.BlockSpec((None, l, HEAD_COLS), lambda s: (finished(s)[0], 0, vcol0 + finished(s)[1])),
            pl.BlockSpec((None, 5, t, t), lambda s: (scored(s)[1], 0, 0, 0)),
            pl.BlockSpec((4, ATTN_HEAD_DIM), lambda s: (0, 0)),
            pl.BlockSpec((1, HEAD_COLS), lambda s: (0, 0)),
        ],
        out_specs=pl.BlockSpec((None, t, HEAD_COLS), lambda s: (finished(s)[0], finished(s)[2], finished(s)[1])),
        out_shape=jax.ShapeDtypeStruct((b, l, ATTN_HEADS * HEAD_COLS), BF16),
        scratch_shapes=[pltpu.VMEM((2, t, l), F32), pltpu.VMEM((2, t, l), F32),
                        pltpu.VMEM((2, t, 1), F32), pltpu.VMEM((2, t, 1), F32),
                        pltpu.VMEM((l, 2 * HEAD_COLS), BF16)],
        compiler_params=_params("arbitrary"),
        name="diff_attention",
    )(qkv3, qkv3, qkv3, bias_tiles, lam_vecs, g_subln)


def _filter_kernel(w1t_ref, w1c_ref, w1s_ref, b1_ref, w2_ref, b2_ref, w3_ref, b3_ref, w4_ref, fr_ref, o_ref,
                   *, l, tl, width):
    i = pl.program_id(0)
    pos = (lax.broadcasted_iota(jnp.int32, (tl, 1), 0) + i * tl).astype(F32)
    t = pos / (l - 1)
    ang = 2.0 * math.pi * pos / l
    bi = lax.broadcasted_iota(jnp.int32, (1, FILTER_BANDS), 1).astype(F32)
    bands = 1e-4 + bi * ((FILTER_BANDS - 1 - 1e-4) / (FILTER_BANDS - 1))
    pre = t * w1t_ref[...] + _dot(jnp.cos(bands * ang), w1c_ref[...]) - _dot(jnp.sin(bands * ang), w1s_ref[...])
    fr = fr_ref[...]
    h = jnp.sin(fr * (pre + b1_ref[...]))
    h = jnp.sin(fr * (_dot(h, w2_ref[...]) + b2_ref[...]))
    h = jnp.sin(fr * (_dot(h, w3_ref[...]) + b3_ref[...]))
    h = _dot(h, w4_ref[...])
    max_decay = math.log(DECAY_TARGET) / FAST_DECAY_PCT
    min_decay = math.log(DECAY_TARGET) / SLOW_DECAY_PCT
    ci = lax.broadcasted_iota(jnp.int32, (1, 2 * width), 1)
    ci = jnp.where(ci >= width, ci - width, ci).astype(F32)
    deltas = jnp.abs(min_decay + ci * ((max_decay - min_decay) / (width - 1)))
    o_ref[...] = h * jnp.exp(-t * deltas)


def _hyena_filter(l, w1, b1, w2, b2, w3, b3, w4, freq, tl):
    width = w4.shape[1] // 2
    full = lambda a: pl.BlockSpec(a.shape, lambda i: (0,) * a.ndim)
    args = (w1[0:1], w1[1:1 + FILTER_BANDS], w1[1 + FILTER_BANDS:], b1, w2, b2, w3, b3, w4, freq)
    return pl.pallas_call(
        functools.partial(_filter_kernel, l=l, tl=tl, width=width),
        grid=(l // tl,),
        in_specs=[full(a) for a in args],
        out_specs=pl.BlockSpec((tl, 2 * width), lambda i: (i, 0)),
        out_shape=jax.ShapeDtypeStruct((l, 2 * width), F32),
        compiler_params=_params("arbitrary"),
        name="hyena_filter",
    )(*args)


def _shifted_rows(z):
    n = z.shape[0]
    row = lax.broadcasted_iota(jnp.int32, z.shape, 0)
    prev = jnp.where(row == 0, 0.0, pltpu.roll(z, 1, 0))
    nxt = jnp.where(row == n - 1, 0.0, pltpu.roll(z, n - 1, 0))
    return prev, nxt


def _dwconv_rows(z, w, b):
    prev, nxt = _shifted_rows(z)
    return prev * w[0:1, :] + z * w[1:2, :] + nxt * w[2:3, :] + b


def _hy_gate_kernel(a0, a1, a2, w0, w1, w2, b0, b1, b2, x0_ref, u_ref):
    x0_ref[...] = _dwconv_rows(a0[...], w0[...], b0[...])
    u = _dwconv_rows(a1[...], w1[...], b1[...]) * _dwconv_rows(a2[...], w2[...], b2[...])
    u_ref[...] = u.astype(u_ref.dtype)


def _hy_gate(hy3, conv_w, conv_b, ct):
    b, l, cols = hy3.shape
    c = cols // 3
    nct = c // ct
    a_spec = lambda k: pl.BlockSpec((None, l, ct), lambda bi, ci: (bi, 0, k * nct + ci))
    w_spec = lambda k: pl.BlockSpec((3, ct), lambda bi, ci: (0, k * nct + ci))
    b_spec = lambda k: pl.BlockSpec((1, ct), lambda bi, ci: (0, k * nct + ci))
    o_spec = pl.BlockSpec((None, l, ct), lambda bi, ci: (bi, 0, ci))
    return pl.pallas_call(
        _hy_gate_kernel,
        grid=(b, nct),
        in_specs=[a_spec(0), a_spec(1), a_spec(2), w_spec(0), w_spec(1), w_spec(2),
                  b_spec(0), b_spec(1), b_spec(2)],
        out_specs=[o_spec, o_spec],
        out_shape=[jax.ShapeDtypeStruct((b, l, c), F32), jax.ShapeDtypeStruct((b, l, c), BF16)],
        compiler_params=_params("arbitrary", "arbitrary"),
        name="hyena_gate",
    )(hy3, hy3, hy3, conv_w, conv_w, conv_w, conv_b, conv_b, conv_b)


FFT_N2 = V7X_MXU_DIM
FFT_R = V7X_BF16_SUBLANES


@functools.lru_cache(maxsize=None)
def _fft_constants(l):
    n = 2 * l
    n2 = FFT_N2
    n1 = n // n2
    assert n1 * n2 == n and n1 % 2 == 0
    r = FFT_R
    eye = np.eye(r)
    k1 = np.arange(n1)[:, None]
    a1 = 2.0 * np.pi * k1 * np.arange(n1 // 2)[None, :] / n1
    m1r = np.kron(np.cos(a1), eye)
    m1i = np.kron(-np.sin(a1), eye)
    m1ir = np.kron(np.cos(a1).T, eye) / n
    m1ii = np.kron(np.sin(a1).T, eye) / n
    a2 = 2.0 * np.pi * np.arange(n2)[:, None] * np.arange(n2)[None, :] / n2
    f2r, f2i = np.cos(a2), -np.sin(a2)
    at = 2.0 * np.pi * k1 * np.arange(n2)[None, :] / n
    twr = np.broadcast_to(np.cos(at)[:, :, None], (n1, n2, V7X_LANES))
    twi = np.broadcast_to(-np.sin(at)[:, :, None], (n1, n2, V7X_LANES))
    f = lambda a: np.ascontiguousarray(a, dtype=np.float32)
    mats = dict(
        m1_real=np.vstack([m1r, m1i]),
        m1_re=np.hstack([m1r, -m1i]), m1_im=np.hstack([m1i, m1r]),
        m1inv_re=np.hstack([m1ir, -m1ii]), m1inv_im=np.hstack([m1ii, m1ir]),
        f2_re=np.hstack([f2r, -f2i]), f2_im=np.hstack([f2i, f2r]),
        f2conj_re=np.hstack([f2r, f2i]), f2conj_im=np.hstack([-f2i, f2r]),
    )
    return dict(n1=n1, n2=n2, mats={k: f(v) for k, v in mats.items()}, tw=(f(twr), f(twi)))


FFT_UNROLL = 4


def _fft_tools(n1, ct, twr_ref, twi_ref, a_ref):
    r = FFT_R
    reps = ct // V7X_LANES
    rows_out = n1 * r

    def rows(ref_slice_fn, count):
        return jnp.concatenate([ref_slice_fn(i) for i in range(count)], axis=0)

    def twiddle(tr, ti):
        if reps == 1:
            return tr, ti
        return jnp.concatenate([tr] * reps, axis=-1), jnp.concatenate([ti] * reps, axis=-1)

    def blk(j):
        return pl.ds(pl.multiple_of(j * r, r), r)

    def strided_fwd(mats, g, j):
        if len(mats) == 1:
            ar, ai = _dot(mats[0][:rows_out], g), _dot(mats[0][rows_out:], g)
        else:
            ar, ai = _dot(mats[0][...], g), _dot(mats[1][...], g)
        tr, ti = twiddle(rows(lambda i: twr_ref[i, blk(j), :], n1), rows(lambda i: twi_ref[i, blk(j), :], n1))
        br = (ar * tr - ai * ti).astype(BF16)
        bi = (ar * ti + ai * tr).astype(BF16)
        for i in range(n1):
            a_ref[0, i, blk(j), :] = br[i * r:(i + 1) * r]
            a_ref[1, i, blk(j), :] = bi[i * r:(i + 1) * r]

    return rows, twiddle, blk, strided_fwd


def _hyena_spectrum_kernel(hf_ref, hb_ref, m1_real_ref, f2_re_ref, f2_im_ref, twr_ref, twi_ref,
                           k_ref, a_ref, *, n1, n2, ct):
    h1 = n1 // 2
    rows, _, blk, strided_fwd = _fft_tools(n1, ct, twr_ref, twi_ref, a_ref)
    for part in range(2):
        def stage1(j, _):
            hf = rows(lambda i: hf_ref[i, blk(j), :], h1)
            hb = rows(lambda i: hb_ref[i, blk(j), :], h1)
            g = (hf + hb) if part == 0 else (hf - hb)
            strided_fwd((m1_real_ref,), g.astype(BF16), j)
            return 0
        lax.fori_loop(0, n2 // FFT_R, stage1, 0, unroll=FFT_UNROLL)

        def stage2(i, _):
            a = jnp.concatenate([a_ref[0, i], a_ref[1, i]], axis=0)
            f2 = f2_re_ref if part == 0 else f2_im_ref
            k_ref[part, i] = _dot(f2[...], a).astype(k_ref.dtype)
            return 0
        lax.fori_loop(0, n1, stage2, 0, unroll=FFT_UNROLL)


def _hyena_conv_kernel(u_ref, k_ref, d_ref, m1_re_ref, m1_im_ref, m1inv_re_ref, m1inv_im_ref,
                       f2_re_ref, f2_im_ref, f2conj_re_ref, f2conj_im_ref, twr_ref, twi_ref,
                       o_ref, a_ref, *, n1, n2, ct):
    r = FFT_R
    h1 = n1 // 2
    nblk = n2 // r
    rows, twiddle, blk, strided_fwd = _fft_tools(n1, ct, twr_ref, twi_ref, a_ref)

    def u_stage1(j, _):
        g = rows(lambda i: u_ref[i // h1, i % h1, blk(j), :], 2 * h1)
        strided_fwd((m1_re_ref, m1_im_ref), g, j)
        return 0
    lax.fori_loop(0, nblk, u_stage1, 0, unroll=FFT_UNROLL)

    def mid(i, _):
        a = jnp.concatenate([a_ref[0, i], a_ref[1, i]], axis=0)
        xr, xi = _dot(f2_re_ref[...], a), _dot(f2_im_ref[...], a)
        kr, ki = k_ref[0, i].astype(F32), k_ref[1, i].astype(F32)
        y = jnp.concatenate([(xr * kr - xi * ki).astype(BF16), (xr * ki + xi * kr).astype(BF16)], axis=0)
        br, bi = _dot(f2conj_re_ref[...], y), _dot(f2conj_im_ref[...], y)
        tr, ti = twiddle(twr_ref[i], twi_ref[i])
        a_ref[0, i] = (br * tr + bi * ti).astype(BF16)
        a_ref[1, i] = (bi * tr - br * ti).astype(BF16)
        return 0
    lax.fori_loop(0, n1, mid, 0, unroll=FFT_UNROLL)

    def last(j, _):
        bb = rows(lambda i: a_ref[i // n1, i % n1, blk(j), :], 2 * n1)
        y = (_dot(m1inv_re_ref[...], bb), _dot(m1inv_im_ref[...], bb))
        d = d_ref[...]
        for bidx in range(2):
            for i in range(h1):
                u = u_ref[bidx, i, blk(j), :].astype(F32)
                o_ref[bidx, i, blk(j), :] = (y[bidx][i * r:(i + 1) * r] + u * d).astype(o_ref.dtype)
        return 0
    lax.fori_loop(0, nblk, last, 0, unroll=FFT_UNROLL)


def _hyena_conv(u, filt, hy_d, ct):
    b, l, c = u.shape
    assert b == 2, "the two batch rows are packed as real / imaginary parts of one complex transform"
    consts = _fft_constants(l)
    n1, n2 = consts["n1"], consts["n2"]
    h1 = n1 // 2
    nct = c // ct
    u4 = u.reshape(b, h1, n2, c)
    f3 = filt.reshape(h1, n2, 2 * c)
    mats = {k: jnp.asarray(v, dtype=BF16) for k, v in consts["mats"].items()}
    spec_consts = [mats[k] for k in ("m1_real", "f2_re", "f2_im")]
    conv_consts = [mats[k] for k in ("m1_re", "m1_im", "m1inv_re", "m1inv_im",
                                     "f2_re", "f2_im", "f2conj_re", "f2conj_im")]
    tws = [jnp.asarray(a) for a in consts["tw"]]
    const_spec = lambda a: pl.BlockSpec(a.shape, lambda ci: (0,) * a.ndim, pipeline_mode=pl.Buffered(1))
    spectrum = pl.pallas_call(
        functools.partial(_hyena_spectrum_kernel, n1=n1, n2=n2, ct=ct),
        grid=(nct,),
        in_specs=[
            pl.BlockSpec((h1, n2, ct), lambda ci: (0, 0, ci)),
            pl.BlockSpec((h1, n2, ct), lambda ci: (0, 0, nct + ci)),
        ] + [const_spec(a) for a in (*spec_consts, *tws)],
        out_specs=pl.BlockSpec((2, n1, n2, ct), lambda ci: (0, 0, 0, ci)),
        out_shape=jax.ShapeDtypeStruct((2, n1, n2, c), BF16),
        scratch_shapes=[pltpu.VMEM((2, n1, n2, ct), BF16)],
        compiler_params=_params("arbitrary"),
        name="hyena_filter_spectrum",
    )(f3, f3, *spec_consts, *tws)
    out = pl.pallas_call(
        functools.partial(_hyena_conv_kernel, n1=n1, n2=n2, ct=ct),
        grid=(nct,),
        in_specs=[
            pl.BlockSpec((b, h1, n2, ct), lambda ci: (0, 0, 0, ci)),
            pl.BlockSpec((2, n1, n2, ct), lambda ci: (0, 0, 0, ci)),
            pl.BlockSpec((1, ct), lambda ci: (0, ci)),
        ] + [const_spec(a) for a in (*conv_consts, *tws)],
        out_specs=pl.BlockSpec((b, h1, n2, ct), lambda ci: (0, 0, 0, ci)),
        out_shape=jax.ShapeDtypeStruct((b, h1, n2, c), BF16),
        scratch_shapes=[pltpu.VMEM((2, n1, n2, ct), BF16)],
        compiler_params=_params("arbitrary"),
        name="hyena_long_conv",
    )(u4, spectrum, hy_d, *conv_consts, *tws)
    return out.reshape(b * l, c)


def _merge_kernel(att_ref, yc_ref, x0_ref, ga_ref, gh_ref, x_ref, wa_ref, wh_ref, wo_ref, g_ref, x1_ref, hf_ref):
    a = _dot(att_ref[...], wa_ref[...])
    yhy = x0_ref[...] * yc_ref[...].astype(F32)
    h = _dot(yhy.astype(BF16), wh_ref[...])
    merged = jax.nn.sigmoid(ga_ref[...]) * a + jax.nn.sigmoid(gh_ref[...]) * h
    x1 = x_ref[...] + _dot(merged.astype(BF16), wo_ref[...])
    x1_ref[...] = x1
    ms = jnp.mean(x1 * x1, axis=-1, keepdims=True)
    hf_ref[...] = (x1 * lax.rsqrt(ms + NORM_EPS) * g_ref[...]).astype(BF16)


def _merge(att, yc, x0, gates, x2, wa, wh, wo, g_ffn, tm):
    m, d = x2.shape
    row = lambda w: pl.BlockSpec((tm, w), lambda i: (i, 0))
    const = lambda a: pl.BlockSpec(a.shape, lambda i: (0, 0), pipeline_mode=pl.Buffered(1))
    return pl.pallas_call(
        _merge_kernel,
        grid=(m // tm,),
        in_specs=[row(att.shape[1]), row(yc.shape[1]), row(x0.shape[1]),
                  pl.BlockSpec((tm, d), lambda i: (i, 0)), pl.BlockSpec((tm, d), lambda i: (i, 1)),
                  row(d), const(wa), const(wh), const(wo), const(g_ffn)],
        out_specs=[row(d), row(d)],
        out_shape=[jax.ShapeDtypeStruct((m, d), F32), jax.ShapeDtypeStruct((m, d), BF16)],
        compiler_params=_params("arbitrary"),
        name="merge_out_proj",
    )(att, yc, x0, gates, gates, x2, wa, wh, wo, g_ffn)


FFN_HALO = V7X_BF16_SUBLANES


def _ffn_kernel(hf_ref, prev_ref, next_ref, x1_ref, wg_ref, wv_ref, cwg_ref, cwv_ref, cbg_ref, cbv_ref,
                wd_ref, g_ref, o_ref, ext_ref, acc_ref, *, tm, tiles_per_seq):
    i = pl.program_id(0)
    f = pl.program_id(1)
    hl = FFN_HALO

    @pl.when(f == 0)
    def _():
        first = (i % tiles_per_seq) == 0
        last = (i % tiles_per_seq) == tiles_per_seq - 1
        ext_ref[0:hl, :] = jnp.where(first, jnp.zeros_like(prev_ref[...]), prev_ref[...])
        ext_ref[hl:hl + tm, :] = hf_ref[...]
        ext_ref[hl + tm:, :] = jnp.where(last, jnp.zeros_like(next_ref[...]), next_ref[...])
        acc_ref[...] = jnp.zeros_like(acc_ref)

    ext = ext_ref[...]
    ne = tm + 2 * hl

    def conv(w_ref, cw_ref, cb_ref):
        up = _dot(ext, w_ref[...])
        cw = cw_ref[...]
        prev = pltpu.roll(up, 1, 0)
        nxt = pltpu.roll(up, ne - 1, 0)
        full = prev * cw[0:1, :] + up * cw[1:2, :] + nxt * cw[2:3, :] + cb_ref[...]
        return full[hl:hl + tm]

    gate = conv(wg_ref, cwg_ref, cbg_ref)
    val = conv(wv_ref, cwv_ref, cbv_ref)
    act = (gate * jax.nn.sigmoid(gate) * val).astype(BF16)
    acc_ref[...] += _dot(act, wd_ref[...])

    @pl.when(f == pl.num_programs(1) - 1)
    def _():
        x2 = x1_ref[...] + acc_ref[...]
        ms = jnp.mean(x2 * x2, axis=-1, keepdims=True)
        o_ref[...] = x2 * lax.rsqrt(ms + NORM_EPS) * g_ref[...]


def _ffn(hf, x1, w_up, conv_w, conv_b, w_down, g_final, l, tm, tf):
    m, d = hf.shape
    dff = w_down.shape[0]
    nf = dff // tf
    hl = FFN_HALO
    assert l % tm == 0 and tm % hl == 0 and dff % tf == 0
    hb = tm // hl
    nhb = m // hl
    return pl.pallas_call(
        functools.partial(_ffn_kernel, tm=tm, tiles_per_seq=l // tm),
        grid=(m // tm, nf),
        in_specs=[
            pl.BlockSpec((tm, d), lambda i, f: (i, 0)),
            pl.BlockSpec((hl, d), lambda i, f: (jnp.maximum(i * hb - 1, 0), 0)),
            pl.BlockSpec((hl, d), lambda i, f: (jnp.minimum((i + 1) * hb, nhb - 1), 0)),
            pl.BlockSpec((tm, d), lambda i, f: (i, 0)),
            pl.BlockSpec((d, tf), lambda i, f: (0, f)),
            pl.BlockSpec((d, tf), lambda i, f: (0, nf + f)),
            pl.BlockSpec((3, tf), lambda i, f: (0, f)),
            pl.BlockSpec((3, tf), lambda i, f: (0, nf + f)),
            pl.BlockSpec((1, tf), lambda i, f: (0, f)),
            pl.BlockSpec((1, tf), lambda i, f: (0, nf + f)),
            pl.BlockSpec((tf, d), lambda i, f: (f, 0)),
            pl.BlockSpec((1, d), lambda i, f: (0, 0)),
        ],
        out_specs=pl.BlockSpec((tm, d), lambda i, f: (i, 0)),
        out_shape=jax.ShapeDtypeStruct((m, d), F32),
        scratch_shapes=[pltpu.VMEM((tm + 2 * hl, d), BF16), pltpu.VMEM((tm, d), F32)],
        compiler_params=_params("arbitrary", "arbitrary"),
        name="conv_ffn",
    )(hf, hf, hf, x1, w_up, w_up, conv_w, conv_w, conv_b, conv_b, w_down, g_final)


def _tile(n, want):
    t = min(n, want)
    assert n % t == 0
    return t


def kernel(x, g_mix, w_in, lambda_q1, lambda_k1, lambda_q2, lambda_k2, g_subln, rel_bias, hy_conv_w,
           hy_conv_b, hy_f_w1, hy_f_b1, hy_f_w2, hy_f_b2, hy_f_w3, hy_f_b3, hy_f_w4, hy_freq, hy_d,
           w_attn_branch, w_hyena_branch, w_out, g_ffn, w_up, ffn_conv_w, ffn_conv_b, w_down, g_final):
    b, l, d = x.shape
    m = b * l
    depth = w_in.shape[0]
    assert depth == 1
    qkv_cols = 3 * ATTN_HEADS * HEAD_COLS
    width = hy_d.shape[1]
    hy_cols = 3 * width
    gate_cols = 2 * d
    assert w_in.shape[2] == qkv_cols + hy_cols + gate_cols

    x2 = x.reshape(m, d)
    row = lambda v: v.reshape(1, -1)

    w_in_b = w_in[0].astype(BF16)
    tm_proj = _tile(m, 1024)
    qkv = _norm_proj(x2, g_mix, w_in_b, 0, qkv_cols, BF16, tm_proj, _tile(qkv_cols, 512))
    hyp = _norm_proj(x2, g_mix, w_in_b, qkv_cols, hy_cols, F32, tm_proj, _tile(hy_cols, 512))
    gates = _norm_proj(x2, g_mix, w_in_b, qkv_cols + hy_cols, gate_cols, F32, tm_proj, _tile(gate_cols, 512))

    t_attn = _tile(l, 256)
    bias_tiles = _bias_tiles(rel_bias, t_attn)
    lam_vecs = jnp.concatenate([lambda_q1, lambda_k1, lambda_q2, lambda_k2], axis=0)
    att = _attention(qkv.reshape(b, l, qkv_cols), bias_tiles, lam_vecs, g_subln, t_attn)
    att = att.reshape(m, ATTN_HEADS * HEAD_COLS)

    filt = _hyena_filter(l, hy_f_w1[0], hy_f_b1, hy_f_w2[0], hy_f_b2, hy_f_w3[0], hy_f_b3, hy_f_w4[0],
                         hy_freq, _tile(l, 512))
    x0, u = _hy_gate(hyp.reshape(b, l, hy_cols), hy_conv_w[0], hy_conv_b, _tile(width, V7X_LANES))
    yc = _hyena_conv(u, filt, hy_d, _tile(width, V7X_MXU_DIM))

    x1, hf = _merge(att, yc, x0.reshape(m, width), gates, x2, w_attn_branch[0].astype(BF16),
                    w_hyena_branch[0].astype(BF16), w_out[0].astype(BF16), g_ffn, _tile(m, 256))

    out = _ffn(hf, x1, w_up[0].astype(BF16), ffn_conv_w[0], ffn_conv_b, w_down[0].astype(BF16),
               row(g_final), l, _tile(l, 512), _tile(w_down.shape[1], 512))
    return out.reshape(b, l, d)
```

```python
import functools
import math

import numpy as np
import jax
import jax.numpy as jnp
from jax import lax
from jax.experimental import pallas as pl
from jax.experimental.pallas import tpu as pltpu

ATTN_HEADS = 8
ATTN_HEAD_DIM = 64
HEAD_COLS = 2 * ATTN_HEAD_DIM
N_BUCKETS = 32
MAX_DISTANCE = 128
NORM_EPS = 1e-6
SUBLN_EPS = 1e-5
LAMBDA_INIT = 0.8 - 0.6 * math.exp(-0.3 * 0)
FILTER_EMB_DIM = 33
FILTER_BANDS = (FILTER_EMB_DIM - 1) // 2
FAST_DECAY_PCT = 0.3
SLOW_DECAY_PCT = 1.5
DECAY_TARGET = 1e-2

V7X_LANES = 128
V7X_BF16_SUBLANES = 16
V7X_MXU_DIM = 256
V7X_VMEM_LIMIT_BYTES = 60 * 1024 * 1024

BF16 = jnp.bfloat16
F32 = jnp.float32


def _params(*sem):
    return pltpu.CompilerParams(dimension_semantics=sem, vmem_limit_bytes=V7X_VMEM_LIMIT_BYTES)


def _dot(a, b):
    return jnp.dot(a, b, preferred_element_type=F32)


def _dot_nt(a, b):
    return lax.dot_general(a, b, (((1,), (1,)), ((), ())), preferred_element_type=F32)


def _norm_proj_kernel(x_ref, g_ref, w_ref, *refs, bounds):
    out_refs, xn_ref = refs[:-1], refs[-1]
    j = pl.program_id(1)

    @pl.when(j == 0)
    def _():
        x = x_ref[...]
        ms = jnp.mean(x * x, axis=-1, keepdims=True)
        xn_ref[...] = (x * lax.rsqrt(ms + NORM_EPS) * g_ref[...]).astype(BF16)

    for o_ref, (lo, hi) in zip(out_refs, bounds):
        @pl.when((j >= lo) & (j < hi))
        def _(o_ref=o_ref):
            o_ref[...] = _dot(xn_ref[...], w_ref[...]).astype(o_ref.dtype)


def _norm_proj(x2, g, w, splits, tm, tn):
    m, d = x2.shape
    assert m % tm == 0 and all(nc % tn == 0 for nc, _ in splits)
    bounds, lo = [], 0
    for nc, _ in splits:
        bounds.append((lo, lo + nc // tn))
        lo += nc // tn
    assert lo * tn == w.shape[1]

    def out_spec(lo, hi):
        return pl.BlockSpec((tm, tn), lambda i, j: (i, jnp.clip(j - lo, 0, hi - lo - 1)))

    return pl.pallas_call(
        functools.partial(_norm_proj_kernel, bounds=tuple(bounds)),
        grid=(m // tm, lo),
        in_specs=[
            pl.BlockSpec((tm, d), lambda i, j: (i, 0)),
            pl.BlockSpec((1, d), lambda i, j: (0, 0)),
            pl.BlockSpec((d, tn), lambda i, j: (0, j)),
        ],
        out_specs=[out_spec(lo_, hi_) for lo_, hi_ in bounds],
        out_shape=[jax.ShapeDtypeStruct((m, nc), dt) for nc, dt in splits],
        scratch_shapes=[pltpu.VMEM((tm, d), BF16)],
        compiler_params=_params("arbitrary", "arbitrary"),
        name="norm_proj",
    )(x2, g, w)


def _t5_large_thresholds():
    half = N_BUCKETS // 2
    max_exact = half // 2
    n = np.arange(max_exact, 4 * MAX_DISTANCE, dtype=np.float64)
    large = max_exact + (np.log(n / max_exact) / math.log(MAX_DISTANCE / max_exact)
                         * (half - max_exact)).astype(np.int64)
    large = np.minimum(large, half - 1)
    thr = [int(n[np.argmax(large >= b)]) for b in range(max_exact + 1, half)]
    assert large[-1] == half - 1 and thr[-1] < MAX_DISTANCE
    return max_exact, half, thr


def _bias_tiles_kernel(rb_ref, o_ref, *, t):
    h = pl.program_id(0)
    max_exact, half, thr = _t5_large_thresholds()
    qi = lax.broadcasted_iota(jnp.int32, (t, t), 0)
    kj = lax.broadcasted_iota(jnp.int32, (t, t), 1)
    for d in range(5):
        rel = (d - 2) * t + kj - qi
        n = jnp.abs(rel)
        large = jnp.full((t, t), max_exact, jnp.int32)
        for th in thr:
            large = large + (n >= th).astype(jnp.int32)
        bucket = jnp.where(rel > 0, half, 0) + jnp.where(n < max_exact, n, large)
        tile = jnp.zeros((t, t), F32)
        for b in range(N_BUCKETS):
            tile = jnp.where(bucket == b, rb_ref[b, h], tile)
        o_ref[d] = tile


def _bias_tiles(rel_bias, t):
    assert t + 1 >= MAX_DISTANCE
    return pl.pallas_call(
        functools.partial(_bias_tiles_kernel, t=t),
        grid=(ATTN_HEADS,),
        in_specs=[pl.BlockSpec(memory_space=pltpu.SMEM)],
        out_specs=pl.BlockSpec((None, 5, t, t), lambda h: (h, 0, 0, 0)),
        out_shape=jax.ShapeDtypeStruct((ATTN_HEADS, 5, t, t), F32),
        compiler_params=_params("arbitrary"),
        name="t5_bias_tiles",
    )(rel_bias)


def _attn_kernel(q_ref, k_ref, v_ref, bias_ref, lam_ref, g_ref, o_ref,
                 sa_ref, sb_ref, ma_ref, mb_ref, vext_ref, *, t, nkb, nq, n_items):
    step = pl.program_id(0)
    qb = jnp.minimum(step, n_items - 1) % nq
    done_item = jnp.maximum(step - 1, 0)

    @pl.when(step == 0)
    def _():
        sb_ref[...] = jnp.zeros_like(sb_ref)
        mb_ref[...] = jnp.zeros_like(mb_ref)

    @pl.when(done_item % nq == 0)
    def _():
        vext_ref[:, :HEAD_COLS] = v_ref[...]
        vext_ref[:, HEAD_COLS:] = jnp.ones((vext_ref.shape[0], HEAD_COLS), BF16)

    def body(cur_s, cur_m, prev_s, prev_m):
        q = q_ref[...] * (ATTN_HEAD_DIM ** -0.5)
        lane = lax.broadcasted_iota(jnp.int32, q.shape, 1)
        qs = (jnp.where(lane < ATTN_HEAD_DIM, q, 0).astype(BF16),
              jnp.where(lane >= ATTN_HEAD_DIM, q, 0).astype(BF16))
        m = [jnp.full((t, 1), -jnp.inf, F32) for _ in range(2)]
        for kb in range(nkb):
            kblk = k_ref[kb * t:(kb + 1) * t, :]
            bias = bias_ref[jnp.clip(kb - qb, -2, 2) + 2]
            for j in range(2):
                s = _dot_nt(qs[j], kblk) + bias
                cur_s[j, :, kb * t:(kb + 1) * t] = s
                m[j] = jnp.maximum(m[j], jnp.max(s, axis=-1, keepdims=True))
        for j in range(2):
            cur_m[j] = m[j]

        lq1, lk1, lq2, lk2 = (lam_ref[i:i + 1, :] for i in range(4))
        lam = (jnp.exp(jnp.sum(lq1 * lk1, keepdims=True)) - jnp.exp(jnp.sum(lq2 * lk2, keepdims=True))
               + LAMBDA_INIT)
        outs = []
        for j in range(2):
            mj = prev_m[j]
            acc = jnp.zeros((t, 2 * HEAD_COLS), F32)
            for kb in range(nkb):
                p = jnp.exp(prev_s[j, :, kb * t:(kb + 1) * t] - mj).astype(BF16)
                acc = acc + _dot(p, vext_ref[kb * t:(kb + 1) * t, :])
            outs.append(acc[:, :HEAD_COLS] / acc[:, HEAD_COLS:HEAD_COLS + 1])
        att = outs[0] - lam * outs[1]
        ms = jnp.mean(att * att, axis=-1, keepdims=True)
        att = att * lax.rsqrt(ms + SUBLN_EPS) * g_ref[...] * (1.0 - LAMBDA_INIT)
        o_ref[...] = att.astype(o_ref.dtype)

    @pl.when(step % 2 == 0)
    def _():
        body(sa_ref, ma_ref, sb_ref, mb_ref)

    @pl.when(step % 2 == 1)
    def _():
        body(sb_ref, mb_ref, sa_ref, ma_ref)


def _attention(qkv3, bias_tiles, lam_vecs, g_subln, t):
    b, l, _ = qkv3.shape
    nkb = nq = l // t
    n_items = b * ATTN_HEADS * nq
    kcol0 = ATTN_HEADS
    vcol0 = 2 * ATTN_HEADS

    def item(s):
        return s // (ATTN_HEADS * nq), (s // nq) % ATTN_HEADS, s % nq

    def scored(s):
        return item(jnp.minimum(s, n_items - 1))

    def finished(s):
        return item(jnp.maximum(s - 1, 0))

    return pl.pallas_call(
        functools.partial(_attn_kernel, t=t, nkb=nkb, nq=nq, n_items=n_items),
        grid=(n_items + 1,),
        in_specs=[
            pl.BlockSpec((None, t, HEAD_COLS), lambda s: (scored(s)[0], scored(s)[2], scored(s)[1])),
            pl.BlockSpec((None, l, HEAD_COLS), lambda s: (scored(s)[0], 0, kcol0 + scored(s)[1])),
            pl.BlockSpec((None, l, HEAD_COLS), lambda s: (finished(s)[0], 0, vcol0 + finished(s)[1])),
            pl.BlockSpec((None, 5, t, t), lambda s: (scored(s)[1], 0, 0, 0)),
            pl.BlockSpec((4, ATTN_HEAD_DIM), lambda s: (0, 0)),
            pl.BlockSpec((1, HEAD_COLS), lambda s: (0, 0)),
        ],
        out_specs=pl.BlockSpec((None, t, HEAD_COLS), lambda s: (finished(s)[0], finished(s)[2], finished(s)[1])),
        out_shape=jax.ShapeDtypeStruct((b, l, ATTN_HEADS * HEAD_COLS), BF16),
        scratch_shapes=[pltpu.VMEM((2, t, l), F32), pltpu.VMEM((2, t, l), F32),
                        pltpu.VMEM((2, t, 1), F32), pltpu.VMEM((2, t, 1), F32),
                        pltpu.VMEM((l, 2 * HEAD_COLS), BF16)],
        compiler_params=_params("arbitrary"),
        name="diff_attention",
    )(qkv3, qkv3, qkv3, bias_tiles, lam_vecs, g_subln)


def _filter_kernel(w1t_ref, w1c_ref, w1s_ref, b1_ref, w2_ref, b2_ref, w3_ref, b3_ref, w4_ref, fr_ref, o_ref,
                   *, l, tl, width):
    i = pl.program_id(0)
    pos = (lax.broadcasted_iota(jnp.int32, (tl, 1), 0) + i * tl).astype(F32)
    t = pos / (l - 1)
    ang = 2.0 * math.pi * pos / l
    bi = lax.broadcasted_iota(jnp.int32, (1, FILTER_BANDS), 1).astype(F32)
    bands = 1e-4 + bi * ((FILTER_BANDS - 1 - 1e-4) / (FILTER_BANDS - 1))
    pre = t * w1t_ref[...] + _dot(jnp.cos(bands * ang), w1c_ref[...]) - _dot(jnp.sin(bands * ang), w1s_ref[...])
    fr = fr_ref[...]
    h = jnp.sin(fr * (pre + b1_ref[...]))
    h = jnp.sin(fr * (_dot(h, w2_ref[...]) + b2_ref[...]))
    h = jnp.sin(fr * (_dot(h, w3_ref[...]) + b3_ref[...]))
    h = _dot(h, w4_ref[...])
    max_decay = math.log(DECAY_TARGET) / FAST_DECAY_PCT
    min_decay = math.log(DECAY_TARGET) / SLOW_DECAY_PCT
    ci = lax.broadcasted_iota(jnp.int32, (1, 2 * width), 1)
    ci = jnp.where(ci >= width, ci - width, ci).astype(F32)
    deltas = jnp.abs(min_decay + ci * ((max_decay - min_decay) / (width - 1)))
    o_ref[...] = h * jnp.exp(-t * deltas)


def _hyena_filter(l, w1, b1, w2, b2, w3, b3, w4, freq, tl):
    width = w4.shape[1] // 2
    full = lambda a: pl.BlockSpec(a.shape, lambda i: (0,) * a.ndim)
    args = (w1[0:1], w1[1:1 + FILTER_BANDS], w1[1 + FILTER_BANDS:], b1, w2, b2, w3, b3, w4, freq)
    return pl.pallas_call(
        functools.partial(_filter_kernel, l=l, tl=tl, width=width),
        grid=(l // tl,),
        in_specs=[full(a) for a in args],
        out_specs=pl.BlockSpec((tl, 2 * width), lambda i: (i, 0)),
        out_shape=jax.ShapeDtypeStruct((l, 2 * width), F32),
        compiler_params=_params("arbitrary"),
        name="hyena_filter",
    )(*args)


def _shifted_rows(z):
    n = z.shape[0]
    row = lax.broadcasted_iota(jnp.int32, z.shape, 0)
    prev = jnp.where(row == 0, 0.0, pltpu.roll(z, 1, 0))
    nxt = jnp.where(row == n - 1, 0.0, pltpu.roll(z, n - 1, 0))
    return prev, nxt


def _dwconv_rows(z, w, b):
    prev, nxt = _shifted_rows(z)
    return prev * w[0:1, :] + z * w[1:2, :] + nxt * w[2:3, :] + b


def _hy_gate_kernel(a0, a1, a2, w0, w1, w2, b0, b1, b2, x0_ref, u_ref):
    x0_ref[...] = _dwconv_rows(a0[...], w0[...], b0[...])
    u = _dwconv_rows(a1[...], w1[...], b1[...]) * _dwconv_rows(a2[...], w2[...], b2[...])
    u_ref[...] = u.astype(u_ref.dtype)


def _hy_gate(hy3, conv_w, conv_b, ct):
    b, l, cols = hy3.shape
    c = cols // 3
    nct = c // ct
    a_spec = lambda k: pl.BlockSpec((None, l, ct), lambda bi, ci: (bi, 0, k * nct + ci))
    w_spec = lambda k: pl.BlockSpec((3, ct), lambda bi, ci: (0, k * nct + ci))
    b_spec = lambda k: pl.BlockSpec((1, ct), lambda bi, ci: (0, k * nct + ci))
    o_spec = pl.BlockSpec((None, l, ct), lambda bi, ci: (bi, 0, ci))
    return pl.pallas_call(
        _hy_gate_kernel,
        grid=(b, nct),
        in_specs=[a_spec(0), a_spec(1), a_spec(2), w_spec(0), w_spec(1), w_spec(2),
                  b_spec(0), b_spec(1), b_spec(2)],
        out_specs=[o_spec, o_spec],
        out_shape=[jax.ShapeDtypeStruct((b, l, c), F32), jax.ShapeDtypeStruct((b, l, c), BF16)],
        compiler_params=_params("arbitrary", "arbitrary"),
        name="hyena_gate",
    )(hy3, hy3, hy3, conv_w, conv_w, conv_w, conv_b, conv_b, conv_b)


FFT_N2 = V7X_MXU_DIM
FFT_R = V7X_BF16_SUBLANES


@functools.lru_cache(maxsize=None)
def _fft_constants(l):
    n = 2 * l
    n2 = FFT_N2
    n1 = n // n2
    assert n1 * n2 == n and n1 % 2 == 0
    r = FFT_R
    eye = np.eye(r)
    k1 = np.arange(n1)[:, None]
    a1 = 2.0 * np.pi * k1 * np.arange(n1 // 2)[None, :] / n1
    m1r = np.kron(np.cos(a1), eye)
    m1i = np.kron(-np.sin(a1), eye)
    m1ir = np.kron(np.cos(a1).T, eye) / n
    m1ii = np.kron(np.sin(a1).T, eye) / n
    a2 = 2.0 * np.pi * np.arange(n2)[:, None] * np.arange(n2)[None, :] / n2
    f2r, f2i = np.cos(a2), -np.sin(a2)
    at = 2.0 * np.pi * k1 * np.arange(n2)[None, :] / n
    twr = np.broadcast_to(np.cos(at)[:, :, None], (n1, n2, V7X_LANES))
    twi = np.broadcast_to(-np.sin(at)[:, :, None], (n1, n2, V7X_LANES))
    f = lambda a: np.ascontiguousarray(a, dtype=np.float32)
    mats = dict(
        m1_real=np.vstack([m1r, m1i]),
        m1_re=np.hstack([m1r, -m1i]), m1_im=np.hstack([m1i, m1r]),
        m1inv_re=np.hstack([m1ir, -m1ii]), m1inv_im=np.hstack([m1ii, m1ir]),
        f2_re=np.hstack([f2r, -f2i]), f2_im=np.hstack([f2i, f2r]),
        f2conj_re=np.hstack([f2r, f2i]), f2conj_im=np.hstack([-f2i, f2r]),
    )
    return dict(n1=n1, n2=n2, mats={k: f(v) for k, v in mats.items()}, tw=(f(twr), f(twi)))


FFT_UNROLL = 4


def _fft_tools(n1, ct, twr_ref, twi_ref, a_ref):
    r = FFT_R
    reps = ct // V7X_LANES
    rows_out = n1 * r

    def rows(ref_slice_fn, count):
        return jnp.concatenate([ref_slice_fn(i) for i in range(count)], axis=0)

    def twiddle(tr, ti):
        if reps == 1:
            return tr, ti
        return jnp.concatenate([tr] * reps, axis=-1), jnp.concatenate([ti] * reps, axis=-1)

    def blk(j):
        return pl.ds(pl.multiple_of(j * r, r), r)

    def strided_fwd(mats, g, j):
        if len(mats) == 1:
            ar, ai = _dot(mats[0][:rows_out], g), _dot(mats[0][rows_out:], g)
        else:
            ar, ai = _dot(mats[0][...], g), _dot(mats[1][...], g)
        tr, ti = twiddle(rows(lambda i: twr_ref[i, blk(j), :], n1), rows(lambda i: twi_ref[i, blk(j), :], n1))
        br = (ar * tr - ai * ti).astype(BF16)
        bi = (ar * ti + ai * tr).astype(BF16)
        for i in range(n1):
            a_ref[0, i, blk(j), :] = br[i * r:(i + 1) * r]
            a_ref[1, i, blk(j), :] = bi[i * r:(i + 1) * r]

    return rows, twiddle, blk, strided_fwd


def _hyena_spectrum_kernel(hf_ref, hb_ref, m1_real_ref, f2_re_ref, f2_im_ref, twr_ref, twi_ref,
                           k_ref, a_ref, *, n1, n2, ct):
    h1 = n1 // 2
    rows, _, blk, strided_fwd = _fft_tools(n1, ct, twr_ref, twi_ref, a_ref)
    for part in range(2):
        def stage1(j, _):
            hf = rows(lambda i: hf_ref[i, blk(j), :], h1)
            hb = rows(lambda i: hb_ref[i, blk(j), :], h1)
            g = (hf + hb) if part == 0 else (hf - hb)
            strided_fwd((m1_real_ref,), g.astype(BF16), j)
            return 0
        lax.fori_loop(0, n2 // FFT_R, stage1, 0, unroll=FFT_UNROLL)

        def stage2(i, _):
            a = jnp.concatenate([a_ref[0, i], a_ref[1, i]], axis=0)
            f2 = f2_re_ref if part == 0 else f2_im_ref
            k_ref[part, i] = _dot(f2[...], a).astype(k_ref.dtype)
            return 0
        lax.fori_loop(0, n1, stage2, 0, unroll=FFT_UNROLL)


def _hyena_conv_kernel(u_ref, k_ref, d_ref, m1_re_ref, m1_im_ref, m1inv_re_ref, m1inv_im_ref,
                       f2_re_ref, f2_im_ref, f2conj_re_ref, f2conj_im_ref, twr_ref, twi_ref,
                       o_ref, a_ref, *, n1, n2, ct):
    r = FFT_R
    h1 = n1 // 2
    nblk = n2 // r
    rows, twiddle, blk, strided_fwd = _fft_tools(n1, ct, twr_ref, twi_ref, a_ref)

    def u_stage1(j, _):
        g = rows(lambda i: u_ref[i // h1, i % h1, blk(j), :], 2 * h1)
        strided_fwd((m1_re_ref, m1_im_ref), g, j)
        return 0
    lax.fori_loop(0, nblk, u_stage1, 0, unroll=FFT_UNROLL)

    def mid(i, _):
        a = jnp.concatenate([a_ref[0, i], a_ref[1, i]], axis=0)
        xr, xi = _dot(f2_re_ref[...], a), _dot(f2_im_ref[...], a)
        kr, ki = k_ref[0, i].astype(F32), k_ref[1, i].astype(F32)
        y = jnp.concatenate([(xr * kr - xi * ki).astype(BF16), (xr * ki + xi * kr).astype(BF16)], axis=0)
        br, bi = _dot(f2conj_re_ref[...], y), _dot(f2conj_im_ref[...], y)
        tr, ti = twiddle(twr_ref[i], twi_ref[i])
        a_ref[0, i] = (br * tr + bi * ti).astype(BF16)
        a_ref[1, i] = (bi * tr - br * ti).astype(BF16)
        return 0
    lax.fori_loop(0, n1, mid, 0, unroll=FFT_UNROLL)

    def last(j, _):
        bb = rows(lambda i: a_ref[i // n1, i % n1, blk(j), :], 2 * n1)
        y = (_dot(m1inv_re_ref[...], bb), _dot(m1inv_im_ref[...], bb))
        d = d_ref[...]
        for bidx in range(2):
            for i in range(h1):
                u = u_ref[bidx, i, blk(j), :].astype(F32)
                o_ref[bidx, i, blk(j), :] = (y[bidx][i * r:(i + 1) * r] + u * d).astype(o_ref.dtype)
        return 0
    lax.fori_loop(0, nblk, last, 0, unroll=FFT_UNROLL)


def _hyena_conv(u, filt, hy_d, ct):
    b, l, c = u.shape
    assert b == 2, "the two batch rows are packed as real / imaginary parts of one complex transform"
    consts = _fft_constants(l)
    n1, n2 = consts["n1"], consts["n2"]
    h1 = n1 // 2
    nct = c // ct
    u4 = u.reshape(b, h1, n2, c)
    f3 = filt.reshape(h1, n2, 2 * c)
    mats = {k: jnp.asarray(v, dtype=BF16) for k, v in consts["mats"].items()}
    spec_consts = [mats[k] for k in ("m1_real", "f2_re", "f2_im")]
    conv_consts = [mats[k] for k in ("m1_re", "m1_im", "m1inv_re", "m1inv_im",
                                     "f2_re", "f2_im", "f2conj_re", "f2conj_im")]
    tws = [jnp.asarray(a) for a in consts["tw"]]
    const_spec = lambda a: pl.BlockSpec(a.shape, lambda ci: (0,) * a.ndim, pipeline_mode=pl.Buffered(1))
    spectrum = pl.pallas_call(
        functools.partial(_hyena_spectrum_kernel, n1=n1, n2=n2, ct=ct),
        grid=(nct,),
        in_specs=[
            pl.BlockSpec((h1, n2, ct), lambda ci: (0, 0, ci)),
            pl.BlockSpec((h1, n2, ct), lambda ci: (0, 0, nct + ci)),
        ] + [const_spec(a) for a in (*spec_consts, *tws)],
        out_specs=pl.BlockSpec((2, n1, n2, ct), lambda ci: (0, 0, 0, ci)),
        out_shape=jax.ShapeDtypeStruct((2, n1, n2, c), BF16),
        scratch_shapes=[pltpu.VMEM((2, n1, n2, ct), BF16)],
        compiler_params=_params("arbitrary"),
        name="hyena_filter_spectrum",
    )(f3, f3, *spec_consts, *tws)
    out = pl.pallas_call(
        functools.partial(_hyena_conv_kernel, n1=n1, n2=n2, ct=ct),
        grid=(nct,),
        in_specs=[
            pl.BlockSpec((b, h1, n2, ct), lambda ci: (0, 0, 0, ci)),
            pl.BlockSpec((2, n1, n2, ct), lambda ci: (0, 0, 0, ci)),
            pl.BlockSpec((1, ct), lambda ci: (0, ci)),
        ] + [const_spec(a) for a in (*conv_consts, *tws)],
        out_specs=pl.BlockSpec((b, h1, n2, ct), lambda ci: (0, 0, 0, ci)),
        out_shape=jax.ShapeDtypeStruct((b, h1, n2, c), BF16),
        scratch_shapes=[pltpu.VMEM((2, n1, n2, ct), BF16)],
        compiler_params=_params("arbitrary"),
        name="hyena_long_conv",
    )(u4, spectrum, hy_d, *conv_consts, *tws)
    return out.reshape(b * l, c)


def _merge_kernel(att_ref, yc_ref, x0_ref, ga_ref, gh_ref, x_ref, wa_ref, wh_ref, wo_ref, g_ref, x1_ref, hf_ref):
    a = _dot(att_ref[...], wa_ref[...])
    yhy = x0_ref[...] * yc_ref[...].astype(F32)
    h = _dot(yhy.astype(BF16), wh_ref[...])
    merged = jax.nn.sigmoid(ga_ref[...]) * a + jax.nn.sigmoid(gh_ref[...]) * h
    x1 = x_ref[...] + _dot(merged.astype(BF16), wo_ref[...])
    x1_ref[...] = x1
    ms = jnp.mean(x1 * x1, axis=-1, keepdims=True)
    hf_ref[...] = (x1 * lax.rsqrt(ms + NORM_EPS) * g_ref[...]).astype(BF16)


def _merge(att, yc, x0, gates, x2, wa, wh, wo, g_ffn, tm):
    m, d = x2.shape
    row = lambda w: pl.BlockSpec((tm, w), lambda i: (i, 0))
    const = lambda a: pl.BlockSpec(a.shape, lambda i: (0, 0), pipeline_mode=pl.Buffered(1))
    return pl.pallas_call(
        _merge_kernel,
        grid=(m // tm,),
        in_specs=[row(att.shape[1]), row(yc.shape[1]), row(x0.shape[1]),
                  pl.BlockSpec((tm, d), lambda i: (i, 0)), pl.BlockSpec((tm, d), lambda i: (i, 1)),
                  row(d), const(wa), const(wh), const(wo), const(g_ffn)],
        out_specs=[row(d), row(d)],
        out_shape=[jax.ShapeDtypeStruct((m, d), F32), jax.ShapeDtypeStruct((m, d), BF16)],
        compiler_params=_params("arbitrary"),
        name="merge_out_proj",
    )(att, yc, x0, gates, gates, x2, wa, wh, wo, g_ffn)


FFN_HALO = V7X_BF16_SUBLANES


def _ffn_kernel(hf_ref, prev_ref, next_ref, x1_ref, wg_ref, wv_ref, cwg_ref, cwv_ref, cbg_ref, cbv_ref,
                wd_ref, g_ref, o_ref, ext_ref, acc_ref, act_a_ref, act_b_ref, *, tm, nf, n_items,
                tiles_per_seq):
    step = pl.program_id(0)
    hl = FFN_HALO
    ne = tm + 2 * hl
    up_item = jnp.minimum(step, n_items - 1)
    i, f = up_item // nf, up_item % nf
    down_f = jnp.maximum(step - 1, 0) % nf

    @pl.when(step == 0)
    def _():
        act_b_ref[...] = jnp.zeros_like(act_b_ref)

    @pl.when(f == 0)
    def _():
        first = (i % tiles_per_seq) == 0
        last = (i % tiles_per_seq) == tiles_per_seq - 1
        ext_ref[0:hl, :] = jnp.where(first, jnp.zeros_like(prev_ref[...]), prev_ref[...])
        ext_ref[hl:hl + tm, :] = hf_ref[...]
        ext_ref[hl + tm:, :] = jnp.where(last, jnp.zeros_like(next_ref[...]), next_ref[...])

    @pl.when(down_f == 0)
    def _():
        acc_ref[...] = jnp.zeros_like(acc_ref)

    def body(cur_act, prev_act):
        ext = ext_ref[...]

        def conv(w_ref, cw_ref, cb_ref):
            up = _dot(ext, w_ref[...])
            cw = cw_ref[...]
            prev = pltpu.roll(up, 1, 0)
            nxt = pltpu.roll(up, ne - 1, 0)
            full = prev * cw[0:1, :] + up * cw[1:2, :] + nxt * cw[2:3, :] + cb_ref[...]
            return full[hl:hl + tm]

        gate = conv(wg_ref, cwg_ref, cbg_ref)
        val = conv(wv_ref, cwv_ref, cbv_ref)
        cur_act[...] = (gate * jax.nn.sigmoid(gate) * val).astype(BF16)
        acc_ref[...] += _dot(prev_act[...], wd_ref[...])

    @pl.when(step % 2 == 0)
    def _():
        body(act_a_ref, act_b_ref)

    @pl.when(step % 2 == 1)
    def _():
        body(act_b_ref, act_a_ref)

    @pl.when((step > 0) & (down_f == nf - 1))
    def _():
        x2 = x1_ref[...] + acc_ref[...]
        ms = jnp.mean(x2 * x2, axis=-1, keepdims=True)
        o_ref[...] = x2 * lax.rsqrt(ms + NORM_EPS) * g_ref[...]


def _ffn(hf, x1, w_up, conv_w, conv_b, w_down, g_final, l, tm, tf):
    m, d = hf.shape
    dff = w_down.shape[0]
    nf = dff // tf
    hl = FFN_HALO
    assert l % tm == 0 and tm % hl == 0 and dff % tf == 0
    hb = tm // hl
    nhb = m // hl
    n_items = (m // tm) * nf

    def up(s):
        item = jnp.minimum(s, n_items - 1)
        return item // nf, item % nf

    def down(s):
        item = jnp.maximum(s - 1, 0)
        return item // nf, item % nf

    return pl.pallas_call(
        functools.partial(_ffn_kernel, tm=tm, nf=nf, n_items=n_items, tiles_per_seq=l // tm),
        grid=(n_items + 1,),
        in_specs=[
            pl.BlockSpec((tm, d), lambda s: (up(s)[0], 0)),
            pl.BlockSpec((hl, d), lambda s: (jnp.maximum(up(s)[0] * hb - 1, 0), 0)),
            pl.BlockSpec((hl, d), lambda s: (jnp.minimum((up(s)[0] + 1) * hb, nhb - 1), 0)),
            pl.BlockSpec((tm, d), lambda s: (down(s)[0], 0)),
            pl.BlockSpec((d, tf), lambda s: (0, up(s)[1])),
            pl.BlockSpec((d, tf), lambda s: (0, nf + up(s)[1])),
            pl.BlockSpec((3, tf), lambda s: (0, up(s)[1])),
            pl.BlockSpec((3, tf), lambda s: (0, nf + up(s)[1])),
            pl.BlockSpec((1, tf), lambda s: (0, up(s)[1])),
            pl.BlockSpec((1, tf), lambda s: (0, nf + up(s)[1])),
            pl.BlockSpec((tf, d), lambda s: (down(s)[1], 0)),
            pl.BlockSpec((1, d), lambda s: (0, 0)),
        ],
        out_specs=pl.BlockSpec((tm, d), lambda s: (down(s)[0], 0)),
        out_shape=jax.ShapeDtypeStruct((m, d), F32),
        scratch_shapes=[pltpu.VMEM((tm + 2 * hl, d), BF16), pltpu.VMEM((tm, d), F32),
                        pltpu.VMEM((tm, tf), BF16), pltpu.VMEM((tm, tf), BF16)],
        compiler_params=_params("arbitrary"),
        name="conv_ffn",
    )(hf, hf, hf, x1, w_up, w_up, conv_w, conv_w, conv_b, conv_b, w_down, g_final)


def _tile(n, want):
    t = min(n, want)
    assert n % t == 0
    return t


def kernel(x, g_mix, w_in, lambda_q1, lambda_k1, lambda_q2, lambda_k2, g_subln, rel_bias, hy_conv_w,
           hy_conv_b, hy_f_w1, hy_f_b1, hy_f_w2, hy_f_b2, hy_f_w3, hy_f_b3, hy_f_w4, hy_freq, hy_d,
           w_attn_branch, w_hyena_branch, w_out, g_ffn, w_up, ffn_conv_w, ffn_conv_b, w_down, g_final):
    b, l, d = x.shape
    m = b * l
    depth = w_in.shape[0]
    assert depth == 1
    qkv_cols = 3 * ATTN_HEADS * HEAD_COLS
    width = hy_d.shape[1]
    hy_cols = 3 * width
    gate_cols = 2 * d
    assert w_in.shape[2] == qkv_cols + hy_cols + gate_cols

    x2 = x.reshape(m, d)
    row = lambda v: v.reshape(1, -1)

    w_in_b = w_in[0].astype(BF16)
    qkv, hyp, gates = _norm_proj(x2, g_mix, w_in_b, ((qkv_cols, BF16), (hy_cols, F32), (gate_cols, F32)),
                                 _tile(m, 1024), math.gcd(qkv_cols, hy_cols, gate_cols, 1024))

    t_attn = _tile(l, 256)
    bias_tiles = _bias_tiles(rel_bias, t_attn)
    lam_vecs = jnp.concatenate([lambda_q1, lambda_k1, lambda_q2, lambda_k2], axis=0)
    att = _attention(qkv.reshape(b, l, qkv_cols), bias_tiles, lam_vecs, g_subln, t_attn)
    att = att.reshape(m, ATTN_HEADS * HEAD_COLS)

    filt = _hyena_filter(l, hy_f_w1[0], hy_f_b1, hy_f_w2[0], hy_f_b2, hy_f_w3[0], hy_f_b3, hy_f_w4[0],
                         hy_freq, _tile(l, 512))
    x0, u = _hy_gate(hyp.reshape(b, l, hy_cols), hy_conv_w[0], hy_conv_b, _tile(width, V7X_LANES))
    yc = _hyena_conv(u, filt, hy_d, _tile(width, V7X_MXU_DIM))

    x1, hf = _merge(att, yc, x0.reshape(m, width), gates, x2, w_attn_branch[0].astype(BF16),
                    w_hyena_branch[0].astype(BF16), w_out[0].astype(BF16), g_ffn, _tile(m, 256))

    out = _ffn(hf, x1, w_up[0].astype(BF16), ffn_conv_w[0], ffn_conv_b, w_down[0].astype(BF16),
               row(g_final), l, _tile(l, 512), _tile(w_down.shape[1], 512))
    return out.reshape(b, l, d)
```

```python
import functools
import math

import numpy as np
import jax
import jax.numpy as jnp
from jax import lax
from jax.experimental import pallas as pl
from jax.experimental.pallas import tpu as pltpu

ATTN_HEADS = 8
ATTN_HEAD_DIM = 64
HEAD_COLS = 2 * ATTN_HEAD_DIM
N_BUCKETS = 32
MAX_DISTANCE = 128
NORM_EPS = 1e-6
SUBLN_EPS = 1e-5
LAMBDA_INIT = 0.8 - 0.6 * math.exp(-0.3 * 0)
FILTER_EMB_DIM = 33
FILTER_BANDS = (FILTER_EMB_DIM - 1) // 2
FAST_DECAY_PCT = 0.3
SLOW_DECAY_PCT = 1.5
DECAY_TARGET = 1e-2

V7X_LANES = 128
V7X_BF16_SUBLANES = 16
V7X_MXU_DIM = 256
V7X_VMEM_LIMIT_BYTES = 60 * 1024 * 1024

BF16 = jnp.bfloat16
F32 = jnp.float32


def _params(*sem):
    return pltpu.CompilerParams(dimension_semantics=sem, vmem_limit_bytes=V7X_VMEM_LIMIT_BYTES)


def _dot(a, b):
    return jnp.dot(a, b, preferred_element_type=F32)


def _dot_nt(a, b):
    return lax.dot_general(a, b, (((1,), (1,)), ((), ())), preferred_element_type=F32)


def _norm_proj_kernel(x_ref, g_ref, w_ref, *refs, bounds):
    out_refs, xn_ref = refs[:-1], refs[-1]
    j = pl.program_id(1)

    @pl.when(j == 0)
    def _():
        x = x_ref[...]
        ms = jnp.mean(x * x, axis=-1, keepdims=True)
        xn_ref[...] = (x * lax.rsqrt(ms + NORM_EPS) * g_ref[...]).astype(BF16)

    for o_ref, (lo, hi) in zip(out_refs, bounds):
        @pl.when((j >= lo) & (j < hi))
        def _(o_ref=o_ref):
            o_ref[...] = _dot(xn_ref[...], w_ref[...]).astype(o_ref.dtype)


def _norm_proj(x2, g, w, splits, tm, tn):
    m, d = x2.shape
    assert m % tm == 0 and all(nc % tn == 0 for nc, _ in splits)
    bounds, lo = [], 0
    for nc, _ in splits:
        bounds.append((lo, lo + nc // tn))
        lo += nc // tn
    assert lo * tn == w.shape[1]

    def out_spec(lo, hi):
        return pl.BlockSpec((tm, tn), lambda i, j: (i, jnp.clip(j - lo, 0, hi - lo - 1)))

    return pl.pallas_call(
        functools.partial(_norm_proj_kernel, bounds=tuple(bounds)),
        grid=(m // tm, lo),
        in_specs=[
            pl.BlockSpec((tm, d), lambda i, j: (i, 0)),
            pl.BlockSpec((1, d), lambda i, j: (0, 0)),
            pl.BlockSpec((d, tn), lambda i, j: (0, j)),
        ],
        out_specs=[out_spec(lo_, hi_) for lo_, hi_ in bounds],
        out_shape=[jax.ShapeDtypeStruct((m, nc), dt) for nc, dt in splits],
        scratch_shapes=[pltpu.VMEM((tm, d), BF16)],
        compiler_params=_params("arbitrary", "arbitrary"),
        name="norm_proj",
    )(x2, g, w)


def _t5_large_thresholds():
    half = N_BUCKETS // 2
    max_exact = half // 2
    n = np.arange(max_exact, 4 * MAX_DISTANCE, dtype=np.float64)
    large = max_exact + (np.log(n / max_exact) / math.log(MAX_DISTANCE / max_exact)
                         * (half - max_exact)).astype(np.int64)
    large = np.minimum(large, half - 1)
    thr = [int(n[np.argmax(large >= b)]) for b in range(max_exact + 1, half)]
    assert large[-1] == half - 1 and thr[-1] < MAX_DISTANCE
    return max_exact, half, thr


def _bias_tiles_kernel(rb_ref, o_ref, *, t):
    h = pl.program_id(0)
    max_exact, half, thr = _t5_large_thresholds()
    qi = lax.broadcasted_iota(jnp.int32, (t, t), 0)
    kj = lax.broadcasted_iota(jnp.int32, (t, t), 1)
    assert t + 1 >= thr[-1]
    for d in (0, 4):
        o_ref[d] = jnp.full((t, t), rb_ref[half - 1 + (half if d == 4 else 0), h], F32)
    for d in (1, 2, 3):
        rel = (d - 2) * t + kj - qi
        n = jnp.abs(rel)
        large = jnp.full((t, t), max_exact, jnp.int32)
        for th in thr:
            large = large + (n >= th).astype(jnp.int32)
        bucket = jnp.where(rel > 0, half, 0) + jnp.where(n < max_exact, n, large)
        tile = jnp.zeros((t, t), F32)
        for b in range(N_BUCKETS):
            tile = jnp.where(bucket == b, rb_ref[b, h], tile)
        o_ref[d] = tile


def _bias_tiles(rel_bias, t):
    assert t + 1 >= MAX_DISTANCE
    return pl.pallas_call(
        functools.partial(_bias_tiles_kernel, t=t),
        grid=(ATTN_HEADS,),
        in_specs=[pl.BlockSpec(memory_space=pltpu.SMEM)],
        out_specs=pl.BlockSpec((None, 5, t, t), lambda h: (h, 0, 0, 0)),
        out_shape=jax.ShapeDtypeStruct((ATTN_HEADS, 5, t, t), F32),
        compiler_params=_params("arbitrary"),
        name="t5_bias_tiles",
    )(rel_bias)


def _attn_kernel(q_ref, k_ref, v_ref, bias_ref, lam_ref, g_ref, o_ref,
                 sa_ref, sb_ref, ma_ref, mb_ref, vext_ref, *, t, nkb, nq, n_items):
    step = pl.program_id(0)
    qb = jnp.minimum(step, n_items - 1) % nq
    done_item = jnp.maximum(step - 1, 0)

    @pl.when(step == 0)
    def _():
        sb_ref[...] = jnp.zeros_like(sb_ref)
        mb_ref[...] = jnp.zeros_like(mb_ref)

    @pl.when(done_item % nq == 0)
    def _():
        vext_ref[:, :HEAD_COLS] = v_ref[...]
        vext_ref[:, HEAD_COLS:] = jnp.ones((vext_ref.shape[0], HEAD_COLS), BF16)

    def body(cur_s, cur_m, prev_s, prev_m):
        q = q_ref[...] * (ATTN_HEAD_DIM ** -0.5)
        lane = lax.broadcasted_iota(jnp.int32, q.shape, 1)
        qs = (jnp.where(lane < ATTN_HEAD_DIM, q, 0).astype(BF16),
              jnp.where(lane >= ATTN_HEAD_DIM, q, 0).astype(BF16))
        m = [jnp.full((t, 1), -jnp.inf, F32) for _ in range(2)]
        for kb in range(nkb):
            kblk = k_ref[kb * t:(kb + 1) * t, :]
            bias = bias_ref[jnp.clip(kb - qb, -2, 2) + 2]
            for j in range(2):
                s = _dot_nt(qs[j], kblk) + bias
                cur_s[j, :, kb * t:(kb + 1) * t] = s
                m[j] = jnp.maximum(m[j], jnp.max(s, axis=-1, keepdims=True))
        for j in range(2):
            cur_m[j] = m[j]

        lq1, lk1, lq2, lk2 = (lam_ref[i:i + 1, :] for i in range(4))
        lam = (jnp.exp(jnp.sum(lq1 * lk1, keepdims=True)) - jnp.exp(jnp.sum(lq2 * lk2, keepdims=True))
               + LAMBDA_INIT)
        outs = []
        for j in range(2):
            mj = prev_m[j]
            acc = jnp.zeros((t, 2 * HEAD_COLS), F32)
            for kb in range(nkb):
                p = jnp.exp(prev_s[j, :, kb * t:(kb + 1) * t] - mj).astype(BF16)
                acc = acc + _dot(p, vext_ref[kb * t:(kb + 1) * t, :])
            outs.append(acc[:, :HEAD_COLS] / acc[:, HEAD_COLS:HEAD_COLS + 1])
        att = outs[0] - lam * outs[1]
        ms = jnp.mean(att * att, axis=-1, keepdims=True)
        att = att * lax.rsqrt(ms + SUBLN_EPS) * g_ref[...] * (1.0 - LAMBDA_INIT)
        o_ref[...] = att.astype(o_ref.dtype)

    @pl.when(step % 2 == 0)
    def _():
        body(sa_ref, ma_ref, sb_ref, mb_ref)

    @pl.when(step % 2 == 1)
    def _():
        body(sb_ref, mb_ref, sa_ref, ma_ref)


def _attention(qkv3, bias_tiles, lam_vecs, g_subln, t):
    b, l, _ = qkv3.shape
    nkb = nq = l // t
    n_items = b * ATTN_HEADS * nq
    kcol0 = ATTN_HEADS
    vcol0 = 2 * ATTN_HEADS

    def item(s):
        return s // (ATTN_HEADS * nq), (s // nq) % ATTN_HEADS, s % nq

    def scored(s):
        return item(jnp.minimum(s, n_items - 1))

    def finished(s):
        return item(jnp.maximum(s - 1, 0))

    return pl.pallas_call(
        functools.partial(_attn_kernel, t=t, nkb=nkb, nq=nq, n_items=n_items),
        grid=(n_items + 1,),
        in_specs=[
            pl.BlockSpec((None, t, HEAD_COLS), lambda s: (scored(s)[0], scored(s)[2], scored(s)[1])),
            pl.BlockSpec((None, l, HEAD_COLS), lambda s: (scored(s)[0], 0, kcol0 + scored(s)[1])),
            pl.BlockSpec((None, l, HEAD_COLS), lambda s: (finished(s)[0], 0, vcol0 + finished(s)[1])),
            pl.BlockSpec((None, 5, t, t), lambda s: (scored(s)[1], 0, 0, 0)),
            pl.BlockSpec((4, ATTN_HEAD_DIM), lambda s: (0, 0)),
            pl.BlockSpec((1, HEAD_COLS), lambda s: (0, 0)),
        ],
        out_specs=pl.BlockSpec((None, t, HEAD_COLS), lambda s: (finished(s)[0], finished(s)[2], finished(s)[1])),
        out_shape=jax.ShapeDtypeStruct((b, l, ATTN_HEADS * HEAD_COLS), BF16),
        scratch_shapes=[pltpu.VMEM((2, t, l), F32), pltpu.VMEM((2, t, l), F32),
                        pltpu.VMEM((2, t, 1), F32), pltpu.VMEM((2, t, 1), F32),
                        pltpu.VMEM((l, 2 * HEAD_COLS), BF16)],
        compiler_params=_params("arbitrary"),
        name="diff_attention",
    )(qkv3, qkv3, qkv3, bias_tiles, lam_vecs, g_subln)


def _filter_kernel(w1t_ref, w1c_ref, w1s_ref, b1_ref, w2_ref, b2_ref, w3_ref, b3_ref, w4_ref, fr_ref, o_ref,
                   *, l, tl, width):
    i = pl.program_id(0)
    pos_row = (lax.broadcasted_iota(jnp.int32, (1, tl), 1) + i * tl).astype(F32)
    ang = 2.0 * math.pi * pos_row / l
    bi = lax.broadcasted_iota(jnp.int32, (FILTER_BANDS, 1), 0).astype(F32)
    bands = 1e-4 + bi * ((FILTER_BANDS - 1 - 1e-4) / (FILTER_BANDS - 1))
    pre = (w1t_ref[...] * (pos_row / (l - 1)) + _dot(w1c_ref[...], jnp.cos(bands * ang))
           - _dot(w1s_ref[...], jnp.sin(bands * ang)))
    fr = fr_ref[...]
    h = jnp.sin(fr * (pre + b1_ref[...]))
    h = jnp.sin(fr * (_dot(w2_ref[...], h) + b2_ref[...]))
    h = jnp.sin(fr * (_dot(w3_ref[...], h) + b3_ref[...]))
    h = lax.dot_general(h, w4_ref[...], (((0,), (0,)), ((), ())), preferred_element_type=F32)
    pos = (lax.broadcasted_iota(jnp.int32, (tl, 1), 0) + i * tl).astype(F32)
    t = pos / (l - 1)
    max_decay = math.log(DECAY_TARGET) / FAST_DECAY_PCT
    min_decay = math.log(DECAY_TARGET) / SLOW_DECAY_PCT
    ci = lax.broadcasted_iota(jnp.int32, (1, 2 * width), 1)
    ci = jnp.where(ci >= width, ci - width, ci).astype(F32)
    deltas = jnp.abs(min_decay + ci * ((max_decay - min_decay) / (width - 1)))
    o_ref[...] = h * jnp.exp(-t * deltas)


def _hyena_filter(l, w1, b1, w2, b2, w3, b3, w4, freq, tl):
    width = w4.shape[1] // 2
    full = lambda a: pl.BlockSpec(a.shape, lambda i: (0,) * a.ndim)
    args = (w1[0:1].T, w1[1:1 + FILTER_BANDS].T, w1[1 + FILTER_BANDS:].T, b1.T, w2.T, b2.T, w3.T, b3.T,
            w4, freq.T)
    return pl.pallas_call(
        functools.partial(_filter_kernel, l=l, tl=tl, width=width),
        grid=(l // tl,),
        in_specs=[full(a) for a in args],
        out_specs=pl.BlockSpec((tl, 2 * width), lambda i: (i, 0)),
        out_shape=jax.ShapeDtypeStruct((l, 2 * width), F32),
        compiler_params=_params("arbitrary"),
        name="hyena_filter",
    )(*args)


def _shifted_rows(z):
    n = z.shape[0]
    g = 8
    row = lax.broadcasted_iota(jnp.int32, (g, z.shape[1]), 0)
    prev = pltpu.roll(z, 1, 0)
    nxt = pltpu.roll(z, n - 1, 0)
    prev = jnp.concatenate([jnp.where(row == 0, 0.0, prev[:g]), prev[g:]], axis=0)
    nxt = jnp.concatenate([nxt[:n - g], jnp.where(row == g - 1, 0.0, nxt[n - g:])], axis=0)
    return prev, nxt


def _dwconv_rows(z, w, b):
    prev, nxt = _shifted_rows(z)
    return prev * w[0:1, :] + z * w[1:2, :] + nxt * w[2:3, :] + b


def _hy_gate_kernel(a0, a1, a2, w0, w1, w2, b0, b1, b2, x0_ref, u_ref):
    x0_ref[...] = _dwconv_rows(a0[...].astype(F32), w0[...], b0[...])
    u = (_dwconv_rows(a1[...].astype(F32), w1[...], b1[...])
         * _dwconv_rows(a2[...].astype(F32), w2[...], b2[...]))
    u_ref[...] = u.astype(u_ref.dtype)


def _hy_gate(hy3, conv_w, conv_b, ct):
    b, l, cols = hy3.shape
    c = cols // 3
    nct = c // ct
    a_spec = lambda k: pl.BlockSpec((None, l, ct), lambda bi, ci: (bi, 0, k * nct + ci))
    w_spec = lambda k: pl.BlockSpec((3, ct), lambda bi, ci: (0, k * nct + ci))
    b_spec = lambda k: pl.BlockSpec((1, ct), lambda bi, ci: (0, k * nct + ci))
    o_spec = pl.BlockSpec((None, l, ct), lambda bi, ci: (bi, 0, ci))
    return pl.pallas_call(
        _hy_gate_kernel,
        grid=(b, nct),
        in_specs=[a_spec(0), a_spec(1), a_spec(2), w_spec(0), w_spec(1), w_spec(2),
                  b_spec(0), b_spec(1), b_spec(2)],
        out_specs=[o_spec, o_spec],
        out_shape=[jax.ShapeDtypeStruct((b, l, c), F32), jax.ShapeDtypeStruct((b, l, c), BF16)],
        compiler_params=_params("arbitrary", "arbitrary"),
        name="hyena_gate",
    )(hy3, hy3, hy3, conv_w, conv_w, conv_w, conv_b, conv_b, conv_b)


FFT_N2 = V7X_MXU_DIM
FFT_R = V7X_BF16_SUBLANES


@functools.lru_cache(maxsize=None)
def _fft_constants(l):
    n = 2 * l
    n2 = FFT_N2
    n1 = n // n2
    assert n1 * n2 == n and n1 % 2 == 0
    r = FFT_R
    eye = np.eye(r)
    k1 = np.arange(n1)[:, None]
    a1 = 2.0 * np.pi * k1 * np.arange(n1 // 2)[None, :] / n1
    m1r = np.kron(np.cos(a1), eye)
    m1i = np.kron(-np.sin(a1), eye)
    m1ir = np.kron(np.cos(a1).T, eye) / n
    m1ii = np.kron(np.sin(a1).T, eye) / n
    a2 = 2.0 * np.pi * np.arange(n2)[:, None] * np.arange(n2)[None, :] / n2
    f2r, f2i = np.cos(a2), -np.sin(a2)
    at = 2.0 * np.pi * k1 * np.arange(n2)[None, :] / n
    twr = np.broadcast_to(np.cos(at)[:, :, None], (n1, n2, V7X_LANES))
    twi = np.broadcast_to(-np.sin(at)[:, :, None], (n1, n2, V7X_LANES))
    f = lambda a: np.ascontiguousarray(a, dtype=np.float32)
    mats = dict(
        m1_real=np.vstack([m1r, m1i]),
        m1_re=np.hstack([m1r, -m1i]), m1_im=np.hstack([m1i, m1r]),
        m1inv_re=np.hstack([m1ir, -m1ii]), m1inv_im=np.hstack([m1ii, m1ir]),
        f2_re=np.hstack([f2r, -f2i]), f2_im=np.hstack([f2i, f2r]),
        f2conj_re=np.hstack([f2r, f2i]), f2conj_im=np.hstack([-f2i, f2r]),
    )
    return dict(n1=n1, n2=n2, mats={k: f(v) for k, v in mats.items()}, tw=(f(twr), f(twi)))


FFT_UNROLL = 4


def _fft_tools(n1, ct, twr_ref, twi_ref, a_ref):
    r = FFT_R
    reps = ct // V7X_LANES
    rows_out = n1 * r

    def rows(ref_slice_fn, count):
        return jnp.concatenate([ref_slice_fn(i) for i in range(count)], axis=0)

    def twiddle(tr, ti):
        if reps == 1:
            return tr, ti
        return jnp.concatenate([tr] * reps, axis=-1), jnp.concatenate([ti] * reps, axis=-1)

    def blk(j):
        return pl.ds(pl.multiple_of(j * r, r), r)

    def strided_fwd(mats, g, j):
        if len(mats) == 1:
            ar, ai = _dot(mats[0][:rows_out], g), _dot(mats[0][rows_out:], g)
        else:
            ar, ai = _dot(mats[0][...], g), _dot(mats[1][...], g)
        tr, ti = twiddle(rows(lambda i: twr_ref[i, blk(j), :], n1), rows(lambda i: twi_ref[i, blk(j), :], n1))
        br = (ar * tr - ai * ti).astype(BF16)
        bi = (ar * ti + ai * tr).astype(BF16)
        for i in range(n1):
            a_ref[0, i, blk(j), :] = br[i * r:(i + 1) * r]
            a_ref[1, i, blk(j), :] = bi[i * r:(i + 1) * r]

    return rows, twiddle, blk, strided_fwd


def _hyena_spectrum_kernel(hf_ref, hb_ref, m1_real_ref, f2_re_ref, f2_im_ref, twr_ref, twi_ref,
                           k_ref, a_ref, *, n1, n2, ct):
    h1 = n1 // 2
    rows, _, blk, strided_fwd = _fft_tools(n1, ct, twr_ref, twi_ref, a_ref)
    for part in range(2):
        def stage1(j, _):
            hf = rows(lambda i: hf_ref[i, blk(j), :], h1)
            hb = rows(lambda i: hb_ref[i, blk(j), :], h1)
            g = (hf + hb) if part == 0 else (hf - hb)
            strided_fwd((m1_real_ref,), g.astype(BF16), j)
            return 0
        lax.fori_loop(0, n2 // FFT_R, stage1, 0, unroll=FFT_UNROLL)

        def stage2(i, _):
            a = jnp.concatenate([a_ref[0, i], a_ref[1, i]], axis=0)
            f2 = f2_re_ref if part == 0 else f2_im_ref
            k_ref[part, i] = _dot(f2[...], a).astype(k_ref.dtype)
            return 0
        lax.fori_loop(0, n1, stage2, 0, unroll=FFT_UNROLL)


def _hyena_conv_kernel(u_ref, k_ref, d_ref, m1_re_ref, m1_im_ref, m1inv_re_ref, m1inv_im_ref,
                       f2_re_ref, f2_im_ref, f2conj_re_ref, f2conj_im_ref, twr_ref, twi_ref,
                       o_ref, a_ref, *, n1, n2, ct):
    r = FFT_R
    h1 = n1 // 2
    nblk = n2 // r
    rows, twiddle, blk, strided_fwd = _fft_tools(n1, ct, twr_ref, twi_ref, a_ref)

    def u_stage1(j, _):
        g = rows(lambda i: u_ref[i // h1, i % h1, blk(j), :], 2 * h1)
        strided_fwd((m1_re_ref, m1_im_ref), g, j)
        return 0
    lax.fori_loop(0, nblk, u_stage1, 0, unroll=FFT_UNROLL)

    def mid(i, _):
        a = jnp.concatenate([a_ref[0, i], a_ref[1, i]], axis=0)
        xr, xi = _dot(f2_re_ref[...], a), _dot(f2_im_ref[...], a)
        kr, ki = k_ref[0, i].astype(F32), k_ref[1, i].astype(F32)
        y = jnp.concatenate([(xr * kr - xi * ki).astype(BF16), (xr * ki + xi * kr).astype(BF16)], axis=0)
        br, bi = _dot(f2conj_re_ref[...], y), _dot(f2conj_im_ref[...], y)
        tr, ti = twiddle(twr_ref[i], twi_ref[i])
        a_ref[0, i] = (br * tr + bi * ti).astype(BF16)
        a_ref[1, i] = (bi * tr - br * ti).astype(BF16)
        return 0
    lax.fori_loop(0, n1, mid, 0, unroll=FFT_UNROLL)

    def last(j, _):
        bb = rows(lambda i: a_ref[i // n1, i % n1, blk(j), :], 2 * n1)
        y = (_dot(m1inv_re_ref[...], bb), _dot(m1inv_im_ref[...], bb))
        d = d_ref[...]
        for bidx in range(2):
            for i in range(h1):
                u = u_ref[bidx, i, blk(j), :].astype(F32)
                o_ref[bidx, i, blk(j), :] = (y[bidx][i * r:(i + 1) * r] + u * d).astype(o_ref.dtype)
        return 0
    lax.fori_loop(0, nblk, last, 0, unroll=FFT_UNROLL)


def _hyena_conv(u, filt, hy_d, ct):
    b, l, c = u.shape
    assert b == 2, "the two batch rows are packed as real / imaginary parts of one complex transform"
    consts = _fft_constants(l)
    n1, n2 = consts["n1"], consts["n2"]
    h1 = n1 // 2
    nct = c // ct
    u4 = u.reshape(b, h1, n2, c)
    f3 = filt.reshape(h1, n2, 2 * c)
    mats = {k: jnp.asarray(v, dtype=BF16) for k, v in consts["mats"].items()}
    spec_consts = [mats[k] for k in ("m1_real", "f2_re", "f2_im")]
    conv_consts = [mats[k] for k in ("m1_re", "m1_im", "m1inv_re", "m1inv_im",
                                     "f2_re", "f2_im", "f2conj_re", "f2conj_im")]
    tws = [jnp.asarray(a) for a in consts["tw"]]
    const_spec = lambda a: pl.BlockSpec(a.shape, lambda ci: (0,) * a.ndim, pipeline_mode=pl.Buffered(1))
    spectrum = pl.pallas_call(
        functools.partial(_hyena_spectrum_kernel, n1=n1, n2=n2, ct=ct),
        grid=(nct,),
        in_specs=[
            pl.BlockSpec((h1, n2, ct), lambda ci: (0, 0, ci)),
            pl.BlockSpec((h1, n2, ct), lambda ci: (0, 0, nct + ci)),
        ] + [const_spec(a) for a in (*spec_consts, *tws)],
        out_specs=pl.BlockSpec((2, n1, n2, ct), lambda ci: (0, 0, 0, ci)),
        out_shape=jax.ShapeDtypeStruct((2, n1, n2, c), BF16),
        scratch_shapes=[pltpu.VMEM((2, n1, n2, ct), BF16)],
        compiler_params=_params("arbitrary"),
        name="hyena_filter_spectrum",
    )(f3, f3, *spec_consts, *tws)
    out = pl.pallas_call(
        functools.partial(_hyena_conv_kernel, n1=n1, n2=n2, ct=ct),
        grid=(nct,),
        in_specs=[
            pl.BlockSpec((b, h1, n2, ct), lambda ci: (0, 0, 0, ci)),
            pl.BlockSpec((2, n1, n2, ct), lambda ci: (0, 0, 0, ci)),
            pl.BlockSpec((1, ct), lambda ci: (0, ci)),
        ] + [const_spec(a) for a in (*conv_consts, *tws)],
        out_specs=pl.BlockSpec((b, h1, n2, ct), lambda ci: (0, 0, 0, ci)),
        out_shape=jax.ShapeDtypeStruct((b, h1, n2, c), BF16),
        scratch_shapes=[pltpu.VMEM((2, n1, n2, ct), BF16)],
        compiler_params=_params("arbitrary"),
        name="hyena_long_conv",
    )(u4, spectrum, hy_d, *conv_consts, *tws)
    return out.reshape(b * l, c)


def _merge_kernel(att_ref, yc_ref, x0_ref, ga_ref, gh_ref, x_ref, wa_ref, wh_ref, wo_ref, g_ref, x1_ref, hf_ref):
    a = _dot(att_ref[...], wa_ref[...])
    yhy = x0_ref[...] * yc_ref[...].astype(F32)
    h = _dot(yhy.astype(BF16), wh_ref[...])
    merged = jax.nn.sigmoid(ga_ref[...]) * a + jax.nn.sigmoid(gh_ref[...]) * h
    x1 = x_ref[...] + _dot(merged.astype(BF16), wo_ref[...])
    x1_ref[...] = x1
    ms = jnp.mean(x1 * x1, axis=-1, keepdims=True)
    hf_ref[...] = (x1 * lax.rsqrt(ms + NORM_EPS) * g_ref[...]).astype(BF16)


def _merge(att, yc, x0, gates, x2, wa, wh, wo, g_ffn, tm):
    m, d = x2.shape
    row = lambda w: pl.BlockSpec((tm, w), lambda i: (i, 0))
    const = lambda a: pl.BlockSpec(a.shape, lambda i: (0, 0), pipeline_mode=pl.Buffered(1))
    return pl.pallas_call(
        _merge_kernel,
        grid=(m // tm,),
        in_specs=[row(att.shape[1]), row(yc.shape[1]), row(x0.shape[1]),
                  pl.BlockSpec((tm, d), lambda i: (i, 0)), pl.BlockSpec((tm, d), lambda i: (i, 1)),
                  row(d), const(wa), const(wh), const(wo), const(g_ffn)],
        out_specs=[row(d), row(d)],
        out_shape=[jax.ShapeDtypeStruct((m, d), F32), jax.ShapeDtypeStruct((m, d), BF16)],
        compiler_params=_params("arbitrary"),
        name="merge_out_proj",
    )(att, yc, x0, gates, gates, x2, wa, wh, wo, g_ffn)


FFN_HALO = V7X_BF16_SUBLANES


def _ffn_kernel(hf_ref, prev_ref, next_ref, x1_ref, wg_ref, wv_ref, cwg_ref, cwv_ref, cbg_ref, cbv_ref,
                wd_ref, g_ref, o_ref, ext_ref, acc_ref, *, tm, tiles_per_seq):
    i = pl.program_id(0)
    f = pl.program_id(1)
    hl = FFN_HALO

    @pl.when(f == 0)
    def _():
        first = (i % tiles_per_seq) == 0
        last = (i % tiles_per_seq) == tiles_per_seq - 1
        ext_ref[0:hl, :] = jnp.where(first, jnp.zeros_like(prev_ref[...]), prev_ref[...])
        ext_ref[hl:hl + tm, :] = hf_ref[...]
        ext_ref[hl + tm:, :] = jnp.where(last, jnp.zeros_like(next_ref[...]), next_ref[...])
        acc_ref[...] = jnp.zeros_like(acc_ref)

    ext = ext_ref[...]
    ne = tm + 2 * hl

    def conv(w_ref, cw_ref, cb_ref):
        up = _dot(ext, w_ref[...])
        cw = cw_ref[...]
        prev = pltpu.roll(up, 1, 0)
        nxt = pltpu.roll(up, ne - 1, 0)
        full = prev * cw[0:1, :] + up * cw[1:2, :] + nxt * cw[2:3, :] + cb_ref[...]
        return full[hl:hl + tm]

    gate = conv(wg_ref, cwg_ref, cbg_ref)
    val = conv(wv_ref, cwv_ref, cbv_ref)
    act = (gate * jax.nn.sigmoid(gate) * val).astype(BF16)
    acc_ref[...] += _dot(act, wd_ref[...])

    @pl.when(f == pl.num_programs(1) - 1)
    def _():
        x2 = x1_ref[...] + acc_ref[...]
        ms = jnp.mean(x2 * x2, axis=-1, keepdims=True)
        o_ref[...] = x2 * lax.rsqrt(ms + NORM_EPS) * g_ref[...]


def _ffn(hf, x1, w_up, conv_w, conv_b, w_down, g_final, l, tm, tf):
    m, d = hf.shape
    dff = w_down.shape[0]
    nf = dff // tf
    hl = FFN_HALO
    assert l % tm == 0 and tm % hl == 0 and dff % tf == 0
    hb = tm // hl
    nhb = m // hl
    return pl.pallas_call(
        functools.partial(_ffn_kernel, tm=tm, tiles_per_seq=l // tm),
        grid=(m // tm, nf),
        in_specs=[
            pl.BlockSpec((tm, d), lambda i, f: (i, 0)),
            pl.BlockSpec((hl, d), lambda i, f: (jnp.maximum(i * hb - 1, 0), 0)),
            pl.BlockSpec((hl, d), lambda i, f: (jnp.minimum((i + 1) * hb, nhb - 1), 0)),
            pl.BlockSpec((tm, d), lambda i, f: (i, 0)),
            pl.BlockSpec((d, tf), lambda i, f: (0, f)),
            pl.BlockSpec((d, tf), lambda i, f: (0, nf + f)),
            pl.BlockSpec((3, tf), lambda i, f: (0, f)),
            pl.BlockSpec((3, tf), lambda i, f: (0, nf + f)),
            pl.BlockSpec((1, tf), lambda i, f: (0, f)),
            pl.BlockSpec((1, tf), lambda i, f: (0, nf + f)),
            pl.BlockSpec((tf, d), lambda i, f: (f, 0)),
            pl.BlockSpec((1, d), lambda i, f: (0, 0)),
        ],
        out_specs=pl.BlockSpec((tm, d), lambda i, f: (i, 0)),
        out_shape=jax.ShapeDtypeStruct((m, d), F32),
        scratch_shapes=[pltpu.VMEM((tm + 2 * hl, d), BF16), pltpu.VMEM((tm, d), F32)],
        compiler_params=_params("arbitrary", "arbitrary"),
        name="conv_ffn",
    )(hf, hf, hf, x1, w_up, w_up, conv_w, conv_w, conv_b, conv_b, w_down, g_final)


def _tile(n, want):
    t = min(n, want)
    assert n % t == 0
    return t


def kernel(x, g_mix, w_in, lambda_q1, lambda_k1, lambda_q2, lambda_k2, g_subln, rel_bias, hy_conv_w,
           hy_conv_b, hy_f_w1, hy_f_b1, hy_f_w2, hy_f_b2, hy_f_w3, hy_f_b3, hy_f_w4, hy_freq, hy_d,
           w_attn_branch, w_hyena_branch, w_out, g_ffn, w_up, ffn_conv_w, ffn_conv_b, w_down, g_final):
    b, l, d = x.shape
    m = b * l
    depth = w_in.shape[0]
    assert depth == 1
    qkv_cols = 3 * ATTN_HEADS * HEAD_COLS
    width = hy_d.shape[1]
    hy_cols = 3 * width
    gate_cols = 2 * d
    assert w_in.shape[2] == qkv_cols + hy_cols + gate_cols

    x2 = x.reshape(m, d)
    row = lambda v: v.reshape(1, -1)

    w_in_b = w_in[0].astype(BF16)
    qkv, hyp, gates = _norm_proj(x2, g_mix, w_in_b, ((qkv_cols, BF16), (hy_cols, BF16), (gate_cols, F32)),
                                 _tile(m, 1024), math.gcd(qkv_cols, hy_cols, gate_cols, 1024))

    t_attn = _tile(l, 256)
    bias_tiles = _bias_tiles(rel_bias, t_attn)
    lam_vecs = jnp.concatenate([lambda_q1, lambda_k1, lambda_q2, lambda_k2], axis=0)
    att = _attention(qkv.reshape(b, l, qkv_cols), bias_tiles, lam_vecs, g_subln, t_attn)
    att = att.reshape(m, ATTN_HEADS * HEAD_COLS)

    filt = _hyena_filter(l, hy_f_w1[0], hy_f_b1, hy_f_w2[0], hy_f_b2, hy_f_w3[0], hy_f_b3, hy_f_w4[0],
                         hy_freq, _tile(l, 512))
    x0, u = _hy_gate(hyp.reshape(b, l, hy_cols), hy_conv_w[0], hy_conv_b, _tile(width, V7X_LANES))
    yc = _hyena_conv(u, filt, hy_d, _tile(width, V7X_MXU_DIM))

    x1, hf = _merge(att, yc, x0.reshape(m, width), gates, x2, w_attn_branch[0].astype(BF16),
                    w_hyena_branch[0].astype(BF16), w_out[0].astype(BF16), g_ffn, _tile(m, 256))

    out = _ffn(hf, x1, w_up[0].astype(BF16), ffn_conv_w[0], ffn_conv_b, w_down[0].astype(BF16),
               row(g_final), l, _tile(l, 512), _tile(w_down.shape[1], 512))
    return out.reshape(b, l, d)
```

```python
import functools
import math

import numpy as np
import jax
import jax.numpy as jnp
from jax import lax
from jax.experimental import pallas as pl
from jax.experimental.pallas import tpu as pltpu

ATTN_HEADS = 8
ATTN_HEAD_DIM = 64
HEAD_COLS = 2 * ATTN_HEAD_DIM
N_BUCKETS = 32
MAX_DISTANCE = 128
NORM_EPS = 1e-6
SUBLN_EPS = 1e-5
LAMBDA_INIT = 0.8 - 0.6 * math.exp(-0.3 * 0)
FILTER_EMB_DIM = 33
FILTER_BANDS = (FILTER_EMB_DIM - 1) // 2
FAST_DECAY_PCT = 0.3
SLOW_DECAY_PCT = 1.5
DECAY_TARGET = 1e-2

V7X_LANES = 128
V7X_BF16_SUBLANES = 16
V7X_MXU_DIM = 256
V7X_VMEM_LIMIT_BYTES = 60 * 1024 * 1024

BF16 = jnp.bfloat16
F32 = jnp.float32


def _params(*sem):
    return pltpu.CompilerParams(dimension_semantics=sem, vmem_limit_bytes=V7X_VMEM_LIMIT_BYTES)


def _dot(a, b):
    return jnp.dot(a, b, preferred_element_type=F32)


def _dot_nt(a, b):
    return lax.dot_general(a, b, (((1,), (1,)), ((), ())), preferred_element_type=F32)


def _norm_proj_kernel(x_ref, g_ref, w_ref, *refs, bounds):
    out_refs, xn_ref = refs[:-1], refs[-1]
    j = pl.program_id(1)

    @pl.when(j == 0)
    def _():
        x = x_ref[...]
        ms = jnp.mean(x * x, axis=-1, keepdims=True)
        xn_ref[...] = (x * lax.rsqrt(ms + NORM_EPS) * g_ref[...]).astype(BF16)

    for o_ref, (lo, hi) in zip(out_refs, bounds):
        @pl.when((j >= lo) & (j < hi))
        def _(o_ref=o_ref):
            o_ref[...] = _dot(xn_ref[...], w_ref[...]).astype(o_ref.dtype)


def _norm_proj(x2, g, w, splits, tm, tn):
    m, d = x2.shape
    assert m % tm == 0 and all(nc % tn == 0 for nc, _ in splits)
    bounds, lo = [], 0
    for nc, _ in splits:
        bounds.append((lo, lo + nc // tn))
        lo += nc // tn
    assert lo * tn == w.shape[1]

    def out_spec(lo, hi):
        return pl.BlockSpec((tm, tn), lambda i, j: (i, jnp.clip(j - lo, 0, hi - lo - 1)))

    return pl.pallas_call(
        functools.partial(_norm_proj_kernel, bounds=tuple(bounds)),
        grid=(m // tm, lo),
        in_specs=[
            pl.BlockSpec((tm, d), lambda i, j: (i, 0)),
            pl.BlockSpec((1, d), lambda i, j: (0, 0)),
            pl.BlockSpec((d, tn), lambda i, j: (0, j)),
        ],
        out_specs=[out_spec(lo_, hi_) for lo_, hi_ in bounds],
        out_shape=[jax.ShapeDtypeStruct((m, nc), dt) for nc, dt in splits],
        scratch_shapes=[pltpu.VMEM((tm, d), BF16)],
        compiler_params=_params("arbitrary", "arbitrary"),
        name="norm_proj",
    )(x2, g, w)


def _t5_large_thresholds():
    half = N_BUCKETS // 2
    max_exact = half // 2
    n = np.arange(max_exact, 4 * MAX_DISTANCE, dtype=np.float64)
    large = max_exact + (np.log(n / max_exact) / math.log(MAX_DISTANCE / max_exact)
                         * (half - max_exact)).astype(np.int64)
    large = np.minimum(large, half - 1)
    thr = [int(n[np.argmax(large >= b)]) for b in range(max_exact + 1, half)]
    assert large[-1] == half - 1 and thr[-1] < MAX_DISTANCE
    return max_exact, half, thr


def _bias_tiles_kernel(rb_ref, o_ref, *, t):
    h = pl.program_id(0)
    max_exact, half, thr = _t5_large_thresholds()
    qi = lax.broadcasted_iota(jnp.int32, (t, t), 0)
    kj = lax.broadcasted_iota(jnp.int32, (t, t), 1)
    assert t + 1 >= thr[-1]
    for d in (0, 4):
        o_ref[d] = jnp.full((t, t), rb_ref[half - 1 + (half if d == 4 else 0), h], F32)
    for d in (1, 2, 3):
        rel = (d - 2) * t + kj - qi
        n = jnp.abs(rel)
        large = jnp.full((t, t), max_exact, jnp.int32)
        for th in thr:
            large = large + (n >= th).astype(jnp.int32)
        bucket = jnp.where(rel > 0, half, 0) + jnp.where(n < max_exact, n, large)
        tile = jnp.zeros((t, t), F32)
        for b in range(N_BUCKETS):
            tile = jnp.where(bucket == b, rb_ref[b, h], tile)
        o_ref[d] = tile


def _bias_tiles(rel_bias, t):
    assert t + 1 >= MAX_DISTANCE
    return pl.pallas_call(
        functools.partial(_bias_tiles_kernel, t=t),
        grid=(ATTN_HEADS,),
        in_specs=[pl.BlockSpec(memory_space=pltpu.SMEM)],
        out_specs=pl.BlockSpec((None, 5, t, t), lambda h: (h, 0, 0, 0)),
        out_shape=jax.ShapeDtypeStruct((ATTN_HEADS, 5, t, t), F32),
        compiler_params=_params("arbitrary"),
        name="t5_bias_tiles",
    )(rel_bias)


def _attn_kernel(q_ref, k_ref, v_ref, bias_ref, lam_ref, g_ref, *refs, t, nkb, nq, n_items, n_cast):
    cast_in, o_ref, cast_out = refs[:n_cast], refs[n_cast], refs[n_cast + 1:2 * n_cast + 1]
    sa_ref, sb_ref, ma_ref, mb_ref, vext_ref = refs[2 * n_cast + 1:]
    step = pl.program_id(0)
    qb = jnp.minimum(step, n_items - 1) % nq
    done_item = jnp.maximum(step - 1, 0)

    @pl.when(step == 0)
    def _():
        sb_ref[...] = jnp.zeros_like(sb_ref)
        mb_ref[...] = jnp.zeros_like(mb_ref)

    @pl.when(done_item % nq == 0)
    def _():
        vext_ref[:, :HEAD_COLS] = v_ref[...]
        vext_ref[:, HEAD_COLS:] = jnp.ones((vext_ref.shape[0], HEAD_COLS), BF16)

    def body(cur_s, cur_m, prev_s, prev_m):
        for w_ref, wb_ref in zip(cast_in, cast_out):
            wb_ref[...] = w_ref[...].astype(BF16)

        q = q_ref[...] * (ATTN_HEAD_DIM ** -0.5)
        lane = lax.broadcasted_iota(jnp.int32, q.shape, 1)
        qs = (jnp.where(lane < ATTN_HEAD_DIM, q, 0).astype(BF16),
              jnp.where(lane >= ATTN_HEAD_DIM, q, 0).astype(BF16))
        m = [jnp.full((t, 1), -jnp.inf, F32) for _ in range(2)]
        for kb in range(nkb):
            kblk = k_ref[kb * t:(kb + 1) * t, :]
            bias = bias_ref[jnp.clip(kb - qb, -2, 2) + 2]
            for j in range(2):
                s = _dot_nt(qs[j], kblk) + bias
                cur_s[j, :, kb * t:(kb + 1) * t] = s
                m[j] = jnp.maximum(m[j], jnp.max(s, axis=-1, keepdims=True))
        for j in range(2):
            cur_m[j] = m[j]

        lq1, lk1, lq2, lk2 = (lam_ref[i:i + 1, :] for i in range(4))
        lam = (jnp.exp(jnp.sum(lq1 * lk1, keepdims=True)) - jnp.exp(jnp.sum(lq2 * lk2, keepdims=True))
               + LAMBDA_INIT)
        outs = []
        for j in range(2):
            mj = prev_m[j]
            acc = jnp.zeros((t, 2 * HEAD_COLS), F32)
            for kb in range(nkb):
                p = jnp.exp(prev_s[j, :, kb * t:(kb + 1) * t] - mj).astype(BF16)
                acc = acc + _dot(p, vext_ref[kb * t:(kb + 1) * t, :])
            outs.append(acc[:, :HEAD_COLS] / acc[:, HEAD_COLS:HEAD_COLS + 1])
        att = outs[0] - lam * outs[1]
        ms = jnp.mean(att * att, axis=-1, keepdims=True)
        att = att * lax.rsqrt(ms + SUBLN_EPS) * g_ref[...] * (1.0 - LAMBDA_INIT)
        o_ref[...] = att.astype(o_ref.dtype)

    @pl.when(step % 2 == 0)
    def _():
        body(sa_ref, ma_ref, sb_ref, mb_ref)

    @pl.when(step % 2 == 1)
    def _():
        body(sb_ref, mb_ref, sa_ref, ma_ref)


def _cast_chunks(w, n_steps):
    rows = w.shape[0]
    per = V7X_BF16_SUBLANES
    while rows % per or rows // per > n_steps:
        per += V7X_BF16_SUBLANES
        assert per <= rows
    return w.reshape(rows // per, per, w.shape[1])


def _attention(qkv3, bias_tiles, lam_vecs, g_subln, t, cast_weights):
    b, l, _ = qkv3.shape
    nkb = nq = l // t
    n_items = b * ATTN_HEADS * nq
    kcol0 = ATTN_HEADS
    vcol0 = 2 * ATTN_HEADS
    chunked = [_cast_chunks(w, n_items + 1) for w in cast_weights]

    def cast_spec(w3):
        n = w3.shape[0]
        return pl.BlockSpec((None,) + w3.shape[1:], lambda s: (jnp.minimum(s, n - 1), 0, 0))

    def item(s):
        return s // (ATTN_HEADS * nq), (s // nq) % ATTN_HEADS, s % nq

    def scored(s):
        return item(jnp.minimum(s, n_items - 1))

    def finished(s):
        return item(jnp.maximum(s - 1, 0))

    outs = pl.pallas_call(
        functools.partial(_attn_kernel, t=t, nkb=nkb, nq=nq, n_items=n_items, n_cast=len(chunked)),
        grid=(n_items + 1,),
        in_specs=[
            pl.BlockSpec((None, t, HEAD_COLS), lambda s: (scored(s)[0], scored(s)[2], scored(s)[1])),
            pl.BlockSpec((None, l, HEAD_COLS), lambda s: (scored(s)[0], 0, kcol0 + scored(s)[1])),
            pl.BlockSpec((None, l, HEAD_COLS), lambda s: (finished(s)[0], 0, vcol0 + finished(s)[1])),
            pl.BlockSpec((None, 5, t, t), lambda s: (scored(s)[1], 0, 0, 0)),
            pl.BlockSpec((4, ATTN_HEAD_DIM), lambda s: (0, 0)),
            pl.BlockSpec((1, HEAD_COLS), lambda s: (0, 0)),
        ] + [cast_spec(w3) for w3 in chunked],
        out_specs=[pl.BlockSpec((None, t, HEAD_COLS),
                                lambda s: (finished(s)[0], finished(s)[2], finished(s)[1]))]
        + [cast_spec(w3) for w3 in chunked],
        out_shape=[jax.ShapeDtypeStruct((b, l, ATTN_HEADS * HEAD_COLS), BF16)]
        + [jax.ShapeDtypeStruct(w3.shape, BF16) for w3 in chunked],
        scratch_shapes=[pltpu.VMEM((2, t, l), F32), pltpu.VMEM((2, t, l), F32),
                        pltpu.VMEM((2, t, 1), F32), pltpu.VMEM((2, t, 1), F32),
                        pltpu.VMEM((l, 2 * HEAD_COLS), BF16)],
        compiler_params=_params("arbitrary"),
        name="diff_attention",
    )(qkv3, qkv3, qkv3, bias_tiles, lam_vecs, g_subln, *chunked)
    return outs[0], [wb.reshape(w.shape) for wb, w in zip(outs[1:], cast_weights)]


def _filter_kernel(w1t_ref, w1c_ref, w1s_ref, b1_ref, w2_ref, b2_ref, w3_ref, b3_ref, w4_ref, fr_ref, o_ref,
                   *, l, tl, width):
    i = pl.program_id(0)
    pos_row = (lax.broadcasted_iota(jnp.int32, (1, tl), 1) + i * tl).astype(F32)
    ang = 2.0 * math.pi * pos_row / l
    bi = lax.broadcasted_iota(jnp.int32, (FILTER_BANDS, 1), 0).astype(F32)
    bands = 1e-4 + bi * ((FILTER_BANDS - 1 - 1e-4) / (FILTER_BANDS - 1))
    pre = (w1t_ref[...] * (pos_row / (l - 1)) + _dot(w1c_ref[...], jnp.cos(bands * ang))
           - _dot(w1s_ref[...], jnp.sin(bands * ang)))
    fr = fr_ref[...]
    h = jnp.sin(fr * (pre + b1_ref[...]))
    h = jnp.sin(fr * (_dot(w2_ref[...], h) + b2_ref[...]))
    h = jnp.sin(fr * (_dot(w3_ref[...], h) + b3_ref[...]))
    h = lax.dot_general(h, w4_ref[...], (((0,), (0,)), ((), ())), preferred_element_type=F32)
    pos = (lax.broadcasted_iota(jnp.int32, (tl, 1), 0) + i * tl).astype(F32)
    t = pos / (l - 1)
    max_decay = math.log(DECAY_TARGET) / FAST_DECAY_PCT
    min_decay = math.log(DECAY_TARGET) / SLOW_DECAY_PCT
    ci = lax.broadcasted_iota(jnp.int32, (1, 2 * width), 1)
    ci = jnp.where(ci >= width, ci - width, ci).astype(F32)
    deltas = jnp.abs(min_decay + ci * ((max_decay - min_decay) / (width - 1)))
    o_ref[...] = h * jnp.exp(-t * deltas)


def _hyena_filter(l, w1, b1, w2, b2, w3, b3, w4, freq, tl):
    width = w4.shape[1] // 2
    full = lambda a: pl.BlockSpec(a.shape, lambda i: (0,) * a.ndim)
    args = (w1[0:1].T, w1[1:1 + FILTER_BANDS].T, w1[1 + FILTER_BANDS:].T, b1.T, w2.T, b2.T, w3.T, b3.T,
            w4, freq.T)
    return pl.pallas_call(
        functools.partial(_filter_kernel, l=l, tl=tl, width=width),
        grid=(l // tl,),
        in_specs=[full(a) for a in args],
        out_specs=pl.BlockSpec((tl, 2 * width), lambda i: (i, 0)),
        out_shape=jax.ShapeDtypeStruct((l, 2 * width), F32),
        compiler_params=_params("arbitrary"),
        name="hyena_filter",
    )(*args)


def _shifted_rows(z):
    n = z.shape[0]
    g = 8
    row = lax.broadcasted_iota(jnp.int32, (g, z.shape[1]), 0)
    prev = pltpu.roll(z, 1, 0)
    nxt = pltpu.roll(z, n - 1, 0)
    prev = jnp.concatenate([jnp.where(row == 0, 0.0, prev[:g]), prev[g:]], axis=0)
    nxt = jnp.concatenate([nxt[:n - g], jnp.where(row == g - 1, 0.0, nxt[n - g:])], axis=0)
    return prev, nxt


def _dwconv_rows(z, w, b):
    prev, nxt = _shifted_rows(z)
    return prev * w[0:1, :] + z * w[1:2, :] + nxt * w[2:3, :] + b


def _hy_gate_kernel(a0, a1, a2, w0, w1, w2, b0, b1, b2, x0_ref, u_ref):
    x0_ref[...] = _dwconv_rows(a0[...].astype(F32), w0[...], b0[...])
    u = (_dwconv_rows(a1[...].astype(F32), w1[...], b1[...])
         * _dwconv_rows(a2[...].astype(F32), w2[...], b2[...]))
    u_ref[...] = u.astype(u_ref.dtype)


def _hy_gate(hy3, conv_w, conv_b, ct):
    b, l, cols = hy3.shape
    c = cols // 3
    nct = c // ct
    a_spec = lambda k: pl.BlockSpec((None, l, ct), lambda bi, ci: (bi, 0, k * nct + ci))
    w_spec = lambda k: pl.BlockSpec((3, ct), lambda bi, ci: (0, k * nct + ci))
    b_spec = lambda k: pl.BlockSpec((1, ct), lambda bi, ci: (0, k * nct + ci))
    o_spec = pl.BlockSpec((None, l, ct), lambda bi, ci: (bi, 0, ci))
    return pl.pallas_call(
        _hy_gate_kernel,
        grid=(b, nct),
        in_specs=[a_spec(0), a_spec(1), a_spec(2), w_spec(0), w_spec(1), w_spec(2),
                  b_spec(0), b_spec(1), b_spec(2)],
        out_specs=[o_spec, o_spec],
        out_shape=[jax.ShapeDtypeStruct((b, l, c), F32), jax.ShapeDtypeStruct((b, l, c), BF16)],
        compiler_params=_params("arbitrary", "arbitrary"),
        name="hyena_gate",
    )(hy3, hy3, hy3, conv_w, conv_w, conv_w, conv_b, conv_b, conv_b)


FFT_N2 = V7X_MXU_DIM
FFT_R = V7X_BF16_SUBLANES


@functools.lru_cache(maxsize=None)
def _fft_constants(l):
    n = 2 * l
    n2 = FFT_N2
    n1 = n // n2
    assert n1 * n2 == n and n1 % 2 == 0
    r = FFT_R
    eye = np.eye(r)
    k1 = np.arange(n1)[:, None]
    a1 = 2.0 * np.pi * k1 * np.arange(n1 // 2)[None, :] / n1
    m1r = np.kron(np.cos(a1), eye)
    m1i = np.kron(-np.sin(a1), eye)
    m1ir = np.kron(np.cos(a1).T, eye) / n
    m1ii = np.kron(np.sin(a1).T, eye) / n
    a2 = 2.0 * np.pi * np.arange(n2)[:, None] * np.arange(n2)[None, :] / n2
    f2r, f2i = np.cos(a2), -np.sin(a2)
    at = 2.0 * np.pi * k1 * np.arange(n2)[None, :] / n
    twr = np.broadcast_to(np.cos(at)[:, :, None], (n1, n2, V7X_LANES))
    twi = np.broadcast_to(-np.sin(at)[:, :, None], (n1, n2, V7X_LANES))
    f = lambda a: np.ascontiguousarray(a, dtype=np.float32)
    mats = dict(
        m1_real=np.vstack([m1r, m1i]),
        m1_re=np.hstack([m1r, -m1i]), m1_im=np.hstack([m1i, m1r]),
        m1inv_re=np.hstack([m1ir, -m1ii]), m1inv_im=np.hstack([m1ii, m1ir]),
        f2_re=np.hstack([f2r, -f2i]), f2_im=np.hstack([f2i, f2r]),
        f2conj_re=np.hstack([f2r, f2i]), f2conj_im=np.hstack([-f2i, f2r]),
    )
    return dict(n1=n1, n2=n2, mats={k: f(v) for k, v in mats.items()}, tw=(f(twr), f(twi)))


FFT_UNROLL = 4


def _fft_tools(n1, ct, twr_ref, twi_ref, a_ref):
    r = FFT_R
    reps = ct // V7X_LANES
    rows_out = n1 * r

    def rows(ref_slice_fn, count):
        return jnp.concatenate([ref_slice_fn(i) for i in range(count)], axis=0)

    def twiddle(tr, ti):
        if reps == 1:
            return tr, ti
        return jnp.concatenate([tr] * reps, axis=-1), jnp.concatenate([ti] * reps, axis=-1)

    def blk(j):
        return pl.ds(pl.multiple_of(j * r, r), r)

    def strided_fwd(mats, g, j):
        if len(mats) == 1:
            ar, ai = _dot(mats[0][:rows_out], g), _dot(mats[0][rows_out:], g)
        else:
            ar, ai = _dot(mats[0][...], g), _dot(mats[1][...], g)
        tr, ti = twiddle(rows(lambda i: twr_ref[i, blk(j), :], n1), rows(lambda i: twi_ref[i, blk(j), :], n1))
        br = (ar * tr - ai * ti).astype(BF16)
        bi = (ar * ti + ai * tr).astype(BF16)
        for i in range(n1):
            a_ref[0, i, blk(j), :] = br[i * r:(i + 1) * r]
            a_ref[1, i, blk(j), :] = bi[i * r:(i + 1) * r]

    return rows, twiddle, blk, strided_fwd


def _hyena_spectrum_kernel(hf_ref, hb_ref, m1_real_ref, f2_re_ref, f2_im_ref, twr_ref, twi_ref,
                           k_ref, a_ref, *, n1, n2, ct):
    h1 = n1 // 2
    rows, _, blk, strided_fwd = _fft_tools(n1, ct, twr_ref, twi_ref, a_ref)
    for part in range(2):
        def stage1(j, _):
            hf = rows(lambda i: hf_ref[i, blk(j), :], h1)
            hb = rows(lambda i: hb_ref[i, blk(j), :], h1)
            g = (hf + hb) if part == 0 else (hf - hb)
            strided_fwd((m1_real_ref,), g.astype(BF16), j)
            return 0
        lax.fori_loop(0, n2 // FFT_R, stage1, 0, unroll=FFT_UNROLL)

        def stage2(i, _):
            a = jnp.concatenate([a_ref[0, i], a_ref[1, i]], axis=0)
            f2 = f2_re_ref if part == 0 else f2_im_ref
            k_ref[part, i] = _dot(f2[...], a).astype(k_ref.dtype)
            return 0
        lax.fori_loop(0, n1, stage2, 0, unroll=FFT_UNROLL)


def _hyena_conv_kernel(u_ref, k_ref, d_ref, m1_re_ref, m1_im_ref, m1inv_re_ref, m1inv_im_ref,
                       f2_re_ref, f2_im_ref, f2conj_re_ref, f2conj_im_ref, twr_ref, twi_ref,
                       o_ref, a_ref, *, n1, n2, ct):
    r = FFT_R
    h1 = n1 // 2
    nblk = n2 // r
    rows, twiddle, blk, strided_fwd = _fft_tools(n1, ct, twr_ref, twi_ref, a_ref)

    def u_stage1(j, _):
        g = rows(lambda i: u_ref[i // h1, i % h1, blk(j), :], 2 * h1)
        strided_fwd((m1_re_ref, m1_im_ref), g, j)
        return 0
    lax.fori_loop(0, nblk, u_stage1, 0, unroll=FFT_UNROLL)

    def mid(i, _):
        a = jnp.concatenate([a_ref[0, i], a_ref[1, i]], axis=0)
        xr, xi = _dot(f2_re_ref[...], a), _dot(f2_im_ref[...], a)
        kr, ki = k_ref[0, i].astype(F32), k_ref[1, i].astype(F32)
        y = jnp.concatenate([(xr * kr - xi * ki).astype(BF16), (xr * ki + xi * kr).astype(BF16)], axis=0)
        br, bi = _dot(f2conj_re_ref[...], y), _dot(f2conj_im_ref[...], y)
        tr, ti = twiddle(twr_ref[i], twi_ref[i])
        a_ref[0, i] = (br * tr + bi * ti).astype(BF16)
        a_ref[1, i] = (bi * tr - br * ti).astype(BF16)
        return 0
    lax.fori_loop(0, n1, mid, 0, unroll=FFT_UNROLL)

    def last(j, _):
        bb = rows(lambda i: a_ref[i // n1, i % n1, blk(j), :], 2 * n1)
        y = (_dot(m1inv_re_ref[...], bb), _dot(m1inv_im_ref[...], bb))
        d = d_ref[...]
        for bidx in range(2):
            for i in range(h1):
                u = u_ref[bidx, i, blk(j), :].astype(F32)
                o_ref[bidx, i, blk(j), :] = (y[bidx][i * r:(i + 1) * r] + u * d).astype(o_ref.dtype)
        return 0
    lax.fori_loop(0, nblk, last, 0, unroll=FFT_UNROLL)


def _hyena_conv(u, filt, hy_d, ct):
    b, l, c = u.shape
    assert b == 2, "the two batch rows are packed as real / imaginary parts of one complex transform"
    consts = _fft_constants(l)
    n1, n2 = consts["n1"], consts["n2"]
    h1 = n1 // 2
    nct = c // ct
    u4 = u.reshape(b, h1, n2, c)
    f3 = filt.reshape(h1, n2, 2 * c)
    mats = {k: jnp.asarray(v, dtype=BF16) for k, v in consts["mats"].items()}
    spec_consts = [mats[k] for k in ("m1_real", "f2_re", "f2_im")]
    conv_consts = [mats[k] for k in ("m1_re", "m1_im", "m1inv_re", "m1inv_im",
                                     "f2_re", "f2_im", "f2conj_re", "f2conj_im")]
    tws = [jnp.asarray(a) for a in consts["tw"]]
    const_spec = lambda a: pl.BlockSpec(a.shape, lambda ci: (0,) * a.ndim, pipeline_mode=pl.Buffered(1))
    spectrum = pl.pallas_call(
        functools.partial(_hyena_spectrum_kernel, n1=n1, n2=n2, ct=ct),
        grid=(nct,),
        in_specs=[
            pl.BlockSpec((h1, n2, ct), lambda ci: (0, 0, ci)),
            pl.BlockSpec((h1, n2, ct), lambda ci: (0, 0, nct + ci)),
        ] + [const_spec(a) for a in (*spec_consts, *tws)],
        out_specs=pl.BlockSpec((2, n1, n2, ct), lambda ci: (0, 0, 0, ci)),
        out_shape=jax.ShapeDtypeStruct((2, n1, n2, c), BF16),
        scratch_shapes=[pltpu.VMEM((2, n1, n2, ct), BF16)],
        compiler_params=_params("arbitrary"),
        name="hyena_filter_spectrum",
    )(f3, f3, *spec_consts, *tws)
    out = pl.pallas_call(
        functools.partial(_hyena_conv_kernel, n1=n1, n2=n2, ct=ct),
        grid=(nct,),
        in_specs=[
            pl.BlockSpec((b, h1, n2, ct), lambda ci: (0, 0, 0, ci)),
            pl.BlockSpec((2, n1, n2, ct), lambda ci: (0, 0, 0, ci)),
            pl.BlockSpec((1, ct), lambda ci: (0, ci)),
        ] + [const_spec(a) for a in (*conv_consts, *tws)],
        out_specs=pl.BlockSpec((b, h1, n2, ct), lambda ci: (0, 0, 0, ci)),
        out_shape=jax.ShapeDtypeStruct((b, h1, n2, c), BF16),
        scratch_shapes=[pltpu.VMEM((2, n1, n2, ct), BF16)],
        compiler_params=_params("arbitrary"),
        name="hyena_long_conv",
    )(u4, spectrum, hy_d, *conv_consts, *tws)
    return out.reshape(b * l, c)


def _merge_kernel(att_ref, yc_ref, x0_ref, ga_ref, gh_ref, x_ref, wa_ref, wh_ref, wo_ref, g_ref, x1_ref, hf_ref):
    a = _dot(att_ref[...], wa_ref[...])
    yhy = x0_ref[...] * yc_ref[...].astype(F32)
    h = _dot(yhy.astype(BF16), wh_ref[...])
    merged = jax.nn.sigmoid(ga_ref[...].astype(F32)) * a + jax.nn.sigmoid(gh_ref[...].astype(F32)) * h
    x1 = x_ref[...] + _dot(merged.astype(BF16), wo_ref[...])
    x1_ref[...] = x1
    ms = jnp.mean(x1 * x1, axis=-1, keepdims=True)
    hf_ref[...] = (x1 * lax.rsqrt(ms + NORM_EPS) * g_ref[...]).astype(BF16)


def _merge(att, yc, x0, gates, x2, wa, wh, wo, g_ffn, tm):
    m, d = x2.shape
    row = lambda w: pl.BlockSpec((tm, w), lambda i: (i, 0))
    const = lambda a: pl.BlockSpec(a.shape, lambda i: (0, 0), pipeline_mode=pl.Buffered(1))
    return pl.pallas_call(
        _merge_kernel,
        grid=(m // tm,),
        in_specs=[row(att.shape[1]), row(yc.shape[1]), row(x0.shape[1]),
                  pl.BlockSpec((tm, d), lambda i: (i, 0)), pl.BlockSpec((tm, d), lambda i: (i, 1)),
                  row(d), const(wa), const(wh), const(wo), const(g_ffn)],
        out_specs=[row(d), row(d)],
        out_shape=[jax.ShapeDtypeStruct((m, d), F32), jax.ShapeDtypeStruct((m, d), BF16)],
        compiler_params=_params("arbitrary"),
        name="merge_out_proj",
    )(att, yc, x0, gates, gates, x2, wa, wh, wo, g_ffn)


FFN_HALO = V7X_BF16_SUBLANES


def _ffn_kernel(hf_ref, prev_ref, next_ref, x1_ref, wg_ref, wv_ref, cwg_ref, cwv_ref, cbg_ref, cbv_ref,
                wd_ref, g_ref, o_ref, ext_ref, acc_ref, *, tm, tiles_per_seq):
    i = pl.program_id(0)
    f = pl.program_id(1)
    hl = FFN_HALO

    @pl.when(f == 0)
    def _():
        first = (i % tiles_per_seq) == 0
        last = (i % tiles_per_seq) == tiles_per_seq - 1
        ext_ref[0:hl, :] = jnp.where(first, jnp.zeros_like(prev_ref[...]), prev_ref[...])
        ext_ref[hl:hl + tm, :] = hf_ref[...]
        ext_ref[hl + tm:, :] = jnp.where(last, jnp.zeros_like(next_ref[...]), next_ref[...])
        acc_ref[...] = jnp.zeros_like(acc_ref)

    ext = ext_ref[...]
    ne = tm + 2 * hl

    def conv(w_ref, cw_ref, cb_ref):
        up = _dot(ext, w_ref[...])
        cw = cw_ref[...]
        prev = pltpu.roll(up, 1, 0)
        nxt = pltpu.roll(up, ne - 1, 0)
        full = prev * cw[0:1, :] + up * cw[1:2, :] + nxt * cw[2:3, :] + cb_ref[...]
        return full[hl:hl + tm]

    gate = conv(wg_ref, cwg_ref, cbg_ref)
    val = conv(wv_ref, cwv_ref, cbv_ref)
    act = (gate * jax.nn.sigmoid(gate) * val).astype(BF16)
    acc_ref[...] += _dot(act, wd_ref[...])

    @pl.when(f == pl.num_programs(1) - 1)
    def _():
        x2 = x1_ref[...] + acc_ref[...]
        ms = jnp.mean(x2 * x2, axis=-1, keepdims=True)
        o_ref[...] = x2 * lax.rsqrt(ms + NORM_EPS) * g_ref[...]


def _ffn(hf, x1, w_up, conv_w, conv_b, w_down, g_final, l, tm, tf):
    m, d = hf.shape
    dff = w_down.shape[0]
    nf = dff // tf
    hl = FFN_HALO
    assert l % tm == 0 and tm % hl == 0 and dff % tf == 0
    hb = tm // hl
    nhb = m // hl
    return pl.pallas_call(
        functools.partial(_ffn_kernel, tm=tm, tiles_per_seq=l // tm),
        grid=(m // tm, nf),
        in_specs=[
            pl.BlockSpec((tm, d), lambda i, f: (i, 0)),
            pl.BlockSpec((hl, d), lambda i, f: (jnp.maximum(i * hb - 1, 0), 0)),
            pl.BlockSpec((hl, d), lambda i, f: (jnp.minimum((i + 1) * hb, nhb - 1), 0)),
            pl.BlockSpec((tm, d), lambda i, f: (i, 0)),
            pl.BlockSpec((d, tf), lambda i, f: (0, f)),
            pl.BlockSpec((d, tf), lambda i, f: (0, nf + f)),
            pl.BlockSpec((3, tf), lambda i, f: (0, f)),
            pl.BlockSpec((3, tf), lambda i, f: (0, nf + f)),
            pl.BlockSpec((1, tf), lambda i, f: (0, f)),
            pl.BlockSpec((1, tf), lambda i, f: (0, nf + f)),
            pl.BlockSpec((tf, d), lambda i, f: (f, 0)),
            pl.BlockSpec((1, d), lambda i, f: (0, 0)),
        ],
        out_specs=pl.BlockSpec((tm, d), lambda i, f: (i, 0)),
        out_shape=jax.ShapeDtypeStruct((m, d), F32),
        scratch_shapes=[pltpu.VMEM((tm + 2 * hl, d), BF16), pltpu.VMEM((tm, d), F32)],
        compiler_params=_params("arbitrary", "arbitrary"),
        name="conv_ffn",
    )(hf, hf, hf, x1, w_up, w_up, conv_w, conv_w, conv_b, conv_b, w_down, g_final)


def _tile(n, want):
    t = min(n, want)
    assert n % t == 0
    return t


def kernel(x, g_mix, w_in, lambda_q1, lambda_k1, lambda_q2, lambda_k2, g_subln, rel_bias, hy_conv_w,
           hy_conv_b, hy_f_w1, hy_f_b1, hy_f_w2, hy_f_b2, hy_f_w3, hy_f_b3, hy_f_w4, hy_freq, hy_d,
           w_attn_branch, w_hyena_branch, w_out, g_ffn, w_up, ffn_conv_w, ffn_conv_b, w_down, g_final):
    b, l, d = x.shape
    m = b * l
    depth = w_in.shape[0]
    assert depth == 1
    qkv_cols = 3 * ATTN_HEADS * HEAD_COLS
    width = hy_d.shape[1]
    hy_cols = 3 * width
    gate_cols = 2 * d
    assert w_in.shape[2] == qkv_cols + hy_cols + gate_cols

    x2 = x.reshape(m, d)
    row = lambda v: v.reshape(1, -1)

    w_in_b = w_in[0].astype(BF16)
    qkv, hyp, gates = _norm_proj(x2, g_mix, w_in_b, ((qkv_cols, BF16), (hy_cols, BF16), (gate_cols, BF16)),
                                 _tile(m, 1024), math.gcd(qkv_cols, hy_cols, gate_cols, 1024))

    t_attn = _tile(l, 256)
    bias_tiles = _bias_tiles(rel_bias, t_attn)
    lam_vecs = jnp.concatenate([lambda_q1, lambda_k1, lambda_q2, lambda_k2], axis=0)
    att, (wa_b, wh_b, wo_b, w_up_b, w_down_b) = _attention(
        qkv.reshape(b, l, qkv_cols), bias_tiles, lam_vecs, g_subln, t_attn,
        (w_attn_branch[0], w_hyena_branch[0], w_out[0], w_up[0], w_down[0]))
    att = att.reshape(m, ATTN_HEADS * HEAD_COLS)

    filt = _hyena_filter(l, hy_f_w1[0], hy_f_b1, hy_f_w2[0], hy_f_b2, hy_f_w3[0], hy_f_b3, hy_f_w4[0],
                         hy_freq, _tile(l, 512))
    x0, u = _hy_gate(hyp.reshape(b, l, hy_cols), hy_conv_w[0], hy_conv_b, _tile(width, V7X_LANES))
    yc = _hyena_conv(u, filt, hy_d, _tile(width, V7X_MXU_DIM))

    x1, hf = _merge(att, yc, x0.reshape(m, width), gates, x2, wa_b, wh_b, wo_b, g_ffn, _tile(m, 512))

    out = _ffn(hf, x1, w_up_b, ffn_conv_w[0], ffn_conv_b, w_down_b,
               row(g_final), l, _tile(l, 512), _tile(w_down.shape[1], 512))
    return out.reshape(b, l, d)
```

```python
import functools
import math

import numpy as np
import jax
import jax.numpy as jnp
from jax import lax
from jax.experimental import pallas as pl
from jax.experimental.pallas import tpu as pltpu

ATTN_HEADS = 8
ATTN_HEAD_DIM = 64
HEAD_COLS = 2 * ATTN_HEAD_DIM
N_BUCKETS = 32
MAX_DISTANCE = 128
NORM_EPS = 1e-6
SUBLN_EPS = 1e-5
LAMBDA_INIT = 0.8 - 0.6 * math.exp(-0.3 * 0)
FILTER_EMB_DIM = 33
FILTER_BANDS = (FILTER_EMB_DIM - 1) // 2
FAST_DECAY_PCT = 0.3
SLOW_DECAY_PCT = 1.5
DECAY_TARGET = 1e-2

V7X_LANES = 128
V7X_BF16_SUBLANES = 16
V7X_MXU_DIM = 256
V7X_VMEM_LIMIT_BYTES = 60 * 1024 * 1024

BF16 = jnp.bfloat16
F32 = jnp.float32


def _params(*sem):
    return pltpu.CompilerParams(dimension_semantics=sem, vmem_limit_bytes=V7X_VMEM_LIMIT_BYTES)


def _dot(a, b):
    return jnp.dot(a, b, preferred_element_type=F32)


def _dot_nt(a, b):
    return lax.dot_general(a, b, (((1,), (1,)), ((), ())), preferred_element_type=F32)


def _norm_proj_kernel(x_ref, g_ref, w_ref, *refs, bounds):
    out_refs, xn_ref = refs[:-1], refs[-1]
    j = pl.program_id(1)

    @pl.when(j == 0)
    def _():
        x = x_ref[...]
        ms = jnp.mean(x * x, axis=-1, keepdims=True)
        xn_ref[...] = (x * lax.rsqrt(ms + NORM_EPS) * g_ref[...]).astype(BF16)

    for o_ref, (lo, hi) in zip(out_refs, bounds):
        @pl.when((j >= lo) & (j < hi))
        def _(o_ref=o_ref):
            o_ref[...] = _dot(xn_ref[...], w_ref[...]).astype(o_ref.dtype)


def _norm_proj(x2, g, w, splits, tm, tn):
    m, d = x2.shape
    assert m % tm == 0 and all(nc % tn == 0 for nc, _ in splits)
    bounds, lo = [], 0
    for nc, _ in splits:
        bounds.append((lo, lo + nc // tn))
        lo += nc // tn
    assert lo * tn == w.shape[1]

    def out_spec(lo, hi):
        return pl.BlockSpec((tm, tn), lambda i, j: (i, jnp.clip(j - lo, 0, hi - lo - 1)))

    return pl.pallas_call(
        functools.partial(_norm_proj_kernel, bounds=tuple(bounds)),
        grid=(m // tm, lo),
        in_specs=[
            pl.BlockSpec((tm, d), lambda i, j: (i, 0)),
            pl.BlockSpec((1, d), lambda i, j: (0, 0)),
            pl.BlockSpec((d, tn), lambda i, j: (0, j)),
        ],
        out_specs=[out_spec(lo_, hi_) for lo_, hi_ in bounds],
        out_shape=[jax.ShapeDtypeStruct((m, nc), dt) for nc, dt in splits],
        scratch_shapes=[pltpu.VMEM((tm, d), BF16)],
        compiler_params=_params("arbitrary", "arbitrary"),
        name="norm_proj",
    )(x2, g, w)


def _t5_large_thresholds():
    half = N_BUCKETS // 2
    max_exact = half // 2
    n = np.arange(max_exact, 4 * MAX_DISTANCE, dtype=np.float64)
    large = max_exact + (np.log(n / max_exact) / math.log(MAX_DISTANCE / max_exact)
                         * (half - max_exact)).astype(np.int64)
    large = np.minimum(large, half - 1)
    thr = [int(n[np.argmax(large >= b)]) for b in range(max_exact + 1, half)]
    assert large[-1] == half - 1 and thr[-1] < MAX_DISTANCE
    return max_exact, half, thr


def _bias_tiles_kernel(rb_ref, o_ref, *, t):
    h = pl.program_id(0)
    max_exact, half, thr = _t5_large_thresholds()
    qi = lax.broadcasted_iota(jnp.int32, (t, t), 0)
    kj = lax.broadcasted_iota(jnp.int32, (t, t), 1)
    assert t + 1 >= thr[-1]
    for d in (0, 4):
        o_ref[d] = jnp.full((t, t), rb_ref[half - 1 + (half if d == 4 else 0), h], F32)
    for d in (1, 2, 3):
        rel = (d - 2) * t + kj - qi
        n = jnp.abs(rel)
        large = jnp.full((t, t), max_exact, jnp.int32)
        for th in thr:
            large = large + (n >= th).astype(jnp.int32)
        bucket = jnp.where(rel > 0, half, 0) + jnp.where(n < max_exact, n, large)
        tile = jnp.zeros((t, t), F32)
        for b in range(N_BUCKETS):
            tile = jnp.where(bucket == b, rb_ref[b, h], tile)
        o_ref[d] = tile


def _bias_tiles(rel_bias, t):
    assert t + 1 >= MAX_DISTANCE
    return pl.pallas_call(
        functools.partial(_bias_tiles_kernel, t=t),
        grid=(ATTN_HEADS,),
        in_specs=[pl.BlockSpec(memory_space=pltpu.SMEM)],
        out_specs=pl.BlockSpec((None, 5, t, t), lambda h: (h, 0, 0, 0)),
        out_shape=jax.ShapeDtypeStruct((ATTN_HEADS, 5, t, t), F32),
        compiler_params=_params("arbitrary"),
        name="t5_bias_tiles",
    )(rel_bias)


def _attn_kernel(q_ref, k_ref, v_ref, bias_ref, lam_ref, g_ref, *refs, t, nkb, nq, n_items, n_cast):
    cast_in, o_ref, cast_out = refs[:n_cast], refs[n_cast], refs[n_cast + 1:2 * n_cast + 1]
    sa_ref, sb_ref, ma_ref, mb_ref, vext_ref = refs[2 * n_cast + 1:]
    step = pl.program_id(0)
    qb = jnp.minimum(step, n_items - 1) % nq
    done_item = jnp.maximum(step - 1, 0)

    @pl.when(step == 0)
    def _():
        sb_ref[...] = jnp.zeros_like(sb_ref)
        mb_ref[...] = jnp.zeros_like(mb_ref)

    @pl.when(done_item % nq == 0)
    def _():
        vext_ref[:, :HEAD_COLS] = v_ref[...]
        vext_ref[:, HEAD_COLS:] = jnp.ones((vext_ref.shape[0], HEAD_COLS), BF16)

    def body(cur_s, cur_m, prev_s, prev_m):
        for w_ref, wb_ref in zip(cast_in, cast_out):
            wb_ref[...] = w_ref[...].astype(BF16)

        q = q_ref[...] * (ATTN_HEAD_DIM ** -0.5)
        lane = lax.broadcasted_iota(jnp.int32, q.shape, 1)
        qs = (jnp.where(lane < ATTN_HEAD_DIM, q, 0).astype(BF16),
              jnp.where(lane >= ATTN_HEAD_DIM, q, 0).astype(BF16))
        m = [jnp.full((t, 1), -jnp.inf, F32) for _ in range(2)]
        for kb in range(nkb):
            kblk = k_ref[kb * t:(kb + 1) * t, :]
            bias = bias_ref[jnp.clip(kb - qb, -2, 2) + 2]
            for j in range(2):
                s = _dot_nt(qs[j], kblk) + bias
                cur_s[j, :, kb * t:(kb + 1) * t] = s
                m[j] = jnp.maximum(m[j], jnp.max(s, axis=-1, keepdims=True))
        for j in range(2):
            cur_m[j] = m[j]

        lq1, lk1, lq2, lk2 = (lam_ref[i:i + 1, :] for i in range(4))
        lam = (jnp.exp(jnp.sum(lq1 * lk1, keepdims=True)) - jnp.exp(jnp.sum(lq2 * lk2, keepdims=True))
               + LAMBDA_INIT)
        outs = []
        for j in range(2):
            mj = prev_m[j]
            acc = jnp.zeros((t, 2 * HEAD_COLS), F32)
            for kb in range(nkb):
                p = jnp.exp(prev_s[j, :, kb * t:(kb + 1) * t] - mj).astype(BF16)
                acc = acc + _dot(p, vext_ref[kb * t:(kb + 1) * t, :])
            outs.append(acc[:, :HEAD_COLS] / acc[:, HEAD_COLS:HEAD_COLS + 1])
        att = outs[0] - lam * outs[1]
        ms = jnp.mean(att * att, axis=-1, keepdims=True)
        att = att * lax.rsqrt(ms + SUBLN_EPS) * g_ref[...] * (1.0 - LAMBDA_INIT)
        o_ref[...] = att.astype(o_ref.dtype)

    @pl.when(step % 2 == 0)
    def _():
        body(sa_ref, ma_ref, sb_ref, mb_ref)

    @pl.when(step % 2 == 1)
    def _():
        body(sb_ref, mb_ref, sa_ref, ma_ref)


def _cast_chunks(w, n_steps):
    rows, cols = w.shape
    per = V7X_BF16_SUBLANES
    while rows % per or rows // per > n_steps:
        per += V7X_BF16_SUBLANES
        assert per <= rows
    ncb = 1
    while (rows // per) * ncb * 2 <= n_steps and cols % (ncb * 2 * V7X_LANES) == 0:
        ncb *= 2
    return w.reshape(rows // per, per, cols), ncb


def _attention(qkv3, bias_tiles, lam_vecs, g_subln, t, cast_weights):
    b, l, _ = qkv3.shape
    nkb = nq = l // t
    n_items = b * ATTN_HEADS * nq
    kcol0 = ATTN_HEADS
    vcol0 = 2 * ATTN_HEADS
    chunked = [_cast_chunks(w, n_items + 1) for w in cast_weights]

    def cast_spec(w3, ncb):
        n = w3.shape[0] * ncb

        def index(s):
            c = jnp.minimum(s, n - 1)
            return c // ncb, 0, c % ncb

        return pl.BlockSpec((None, w3.shape[1], w3.shape[2] // ncb), index)

    def item(s):
        return s // (ATTN_HEADS * nq), (s // nq) % ATTN_HEADS, s % nq

    def scored(s):
        return item(jnp.minimum(s, n_items - 1))

    def finished(s):
        return item(jnp.maximum(s - 1, 0))

    outs = pl.pallas_call(
        functools.partial(_attn_kernel, t=t, nkb=nkb, nq=nq, n_items=n_items, n_cast=len(chunked)),
        grid=(n_items + 1,),
        in_specs=[
            pl.BlockSpec((None, t, HEAD_COLS), lambda s: (scored(s)[0], scored(s)[2], scored(s)[1])),
            pl.BlockSpec((None, l, HEAD_COLS), lambda s: (scored(s)[0], 0, kcol0 + scored(s)[1])),
            pl.BlockSpec((None, l, HEAD_COLS), lambda s: (finished(s)[0], 0, vcol0 + finished(s)[1])),
            pl.BlockSpec((None, 5, t, t), lambda s: (scored(s)[1], 0, 0, 0)),
            pl.BlockSpec((4, ATTN_HEAD_DIM), lambda s: (0, 0)),
            pl.BlockSpec((1, HEAD_COLS), lambda s: (0, 0)),
        ] + [cast_spec(w3, ncb) for w3, ncb in chunked],
        out_specs=[pl.BlockSpec((None, t, HEAD_COLS),
                                lambda s: (finished(s)[0], finished(s)[2], finished(s)[1]))]
        + [cast_spec(w3, ncb) for w3, ncb in chunked],
        out_shape=[jax.ShapeDtypeStruct((b, l, ATTN_HEADS * HEAD_COLS), BF16)]
        + [jax.ShapeDtypeStruct(w3.shape, BF16) for w3, _ in chunked],
        scratch_shapes=[pltpu.VMEM((2, t, l), F32), pltpu.VMEM((2, t, l), F32),
                        pltpu.VMEM((2, t, 1), F32), pltpu.VMEM((2, t, 1), F32),
                        pltpu.VMEM((l, 2 * HEAD_COLS), BF16)],
        compiler_params=_params("arbitrary"),
        name="diff_attention",
    )(qkv3, qkv3, qkv3, bias_tiles, lam_vecs, g_subln, *[w3 for w3, _ in chunked])
    return outs[0], [wb.reshape(w.shape) for wb, w in zip(outs[1:], cast_weights)]


def _filter_kernel(w1t_ref, w1c_ref, w1s_ref, b1_ref, w2_ref, b2_ref, w3_ref, b3_ref, w4_ref, fr_ref,
                   cast_ref, o_ref, cast_out_ref, *, l, tl, width):
    cast_out_ref[...] = cast_ref[...].astype(BF16)
    i = pl.program_id(0)
    pos_row = (lax.broadcasted_iota(jnp.int32, (1, tl), 1) + i * tl).astype(F32)
    ang = 2.0 * math.pi * pos_row / l
    bi = lax.broadcasted_iota(jnp.int32, (FILTER_BANDS, 1), 0).astype(F32)
    bands = 1e-4 + bi * ((FILTER_BANDS - 1 - 1e-4) / (FILTER_BANDS - 1))
    pre = (w1t_ref[...] * (pos_row / (l - 1)) + _dot(w1c_ref[...], jnp.cos(bands * ang))
           - _dot(w1s_ref[...], jnp.sin(bands * ang)))
    fr = fr_ref[...]
    h = jnp.sin(fr * (pre + b1_ref[...]))
    h = jnp.sin(fr * (_dot(w2_ref[...], h) + b2_ref[...]))
    h = jnp.sin(fr * (_dot(w3_ref[...], h) + b3_ref[...]))
    h = lax.dot_general(h, w4_ref[...], (((0,), (0,)), ((), ())), preferred_element_type=F32)
    pos = (lax.broadcasted_iota(jnp.int32, (tl, 1), 0) + i * tl).astype(F32)
    t = pos / (l - 1)
    max_decay = math.log(DECAY_TARGET) / FAST_DECAY_PCT
    min_decay = math.log(DECAY_TARGET) / SLOW_DECAY_PCT
    ci = lax.broadcasted_iota(jnp.int32, (1, 2 * width), 1)
    ci = jnp.where(ci >= width, ci - width, ci).astype(F32)
    deltas = jnp.abs(min_decay + ci * ((max_decay - min_decay) / (width - 1)))
    o_ref[...] = h * jnp.exp(-t * deltas)


def _hyena_filter(l, w1, b1, w2, b2, w3, b3, w4, freq, tl, cast_weight):
    width = w4.shape[1] // 2
    steps = l // tl
    rows, cols = cast_weight.shape
    assert rows % (steps * V7X_BF16_SUBLANES) == 0
    cast_spec = pl.BlockSpec((rows // steps, cols), lambda i: (i, 0))
    full = lambda a: pl.BlockSpec(a.shape, lambda i: (0,) * a.ndim)
    args = (w1[0:1].T, w1[1:1 + FILTER_BANDS].T, w1[1 + FILTER_BANDS:].T, b1.T, w2.T, b2.T, w3.T, b3.T,
            w4, freq.T)
    return pl.pallas_call(
        functools.partial(_filter_kernel, l=l, tl=tl, width=width),
        grid=(steps,),
        in_specs=[full(a) for a in args] + [cast_spec],
        out_specs=[pl.BlockSpec((tl, 2 * width), lambda i: (i, 0)), cast_spec],
        out_shape=[jax.ShapeDtypeStruct((l, 2 * width), F32), jax.ShapeDtypeStruct((rows, cols), BF16)],
        compiler_params=_params("arbitrary"),
        name="hyena_filter",
    )(*args, cast_weight)


def _shifted_rows(z):
    n = z.shape[0]
    g = 8
    row = lax.broadcasted_iota(jnp.int32, (g, z.shape[1]), 0)
    prev = pltpu.roll(z, 1, 0)
    nxt = pltpu.roll(z, n - 1, 0)
    prev = jnp.concatenate([jnp.where(row == 0, 0.0, prev[:g]), prev[g:]], axis=0)
    nxt = jnp.concatenate([nxt[:n - g], jnp.where(row == g - 1, 0.0, nxt[n - g:])], axis=0)
    return prev, nxt


def _dwconv_rows(z, w, b):
    prev, nxt = _shifted_rows(z)
    return prev * w[0:1, :] + z * w[1:2, :] + nxt * w[2:3, :] + b


def _hy_gate_kernel(a0, a1, a2, w0, w1, w2, b0, b1, b2, x0_ref, u_ref):
    x0_ref[...] = _dwconv_rows(a0[...].astype(F32), w0[...], b0[...])
    u = (_dwconv_rows(a1[...].astype(F32), w1[...], b1[...])
         * _dwconv_rows(a2[...].astype(F32), w2[...], b2[...]))
    u_ref[...] = u.astype(u_ref.dtype)


def _hy_gate(hy3, conv_w, conv_b, ct):
    b, l, cols = hy3.shape
    c = cols // 3
    nct = c // ct
    a_spec = lambda k: pl.BlockSpec((None, l, ct), lambda bi, ci: (bi, 0, k * nct + ci))
    w_spec = lambda k: pl.BlockSpec((3, ct), lambda bi, ci: (0, k * nct + ci))
    b_spec = lambda k: pl.BlockSpec((1, ct), lambda bi, ci: (0, k * nct + ci))
    o_spec = pl.BlockSpec((None, l, ct), lambda bi, ci: (bi, 0, ci))
    return pl.pallas_call(
        _hy_gate_kernel,
        grid=(b, nct),
        in_specs=[a_spec(0), a_spec(1), a_spec(2), w_spec(0), w_spec(1), w_spec(2),
                  b_spec(0), b_spec(1), b_spec(2)],
        out_specs=[o_spec, o_spec],
        out_shape=[jax.ShapeDtypeStruct((b, l, c), F32), jax.ShapeDtypeStruct((b, l, c), BF16)],
        compiler_params=_params("arbitrary", "arbitrary"),
        name="hyena_gate",
    )(hy3, hy3, hy3, conv_w, conv_w, conv_w, conv_b, conv_b, conv_b)


FFT_N2 = V7X_MXU_DIM
FFT_R = V7X_BF16_SUBLANES


@functools.lru_cache(maxsize=None)
def _fft_constants(l):
    n = 2 * l
    n2 = FFT_N2
    n1 = n // n2
    assert n1 * n2 == n and n1 % 2 == 0
    r = FFT_R
    eye = np.eye(r)
    k1 = np.arange(n1)[:, None]
    a1 = 2.0 * np.pi * k1 * np.arange(n1 // 2)[None, :] / n1
    m1r = np.kron(np.cos(a1), eye)
    m1i = np.kron(-np.sin(a1), eye)
    m1ir = np.kron(np.cos(a1).T, eye) / n
    m1ii = np.kron(np.sin(a1).T, eye) / n
    a2 = 2.0 * np.pi * np.arange(n2)[:, None] * np.arange(n2)[None, :] / n2
    f2r, f2i = np.cos(a2), -np.sin(a2)
    at = 2.0 * np.pi * k1 * np.arange(n2)[None, :] / n
    twr = np.broadcast_to(np.cos(at)[:, :, None], (n1, n2, V7X_LANES))
    twi = np.broadcast_to(-np.sin(at)[:, :, None], (n1, n2, V7X_LANES))
    f = lambda a: np.ascontiguousarray(a, dtype=np.float32)
    mats = dict(
        m1_real=np.vstack([m1r, m1i]),
        m1_re=np.hstack([m1r, -m1i]), m1_im=np.hstack([m1i, m1r]),
        m1inv_re=np.hstack([m1ir, -m1ii]), m1inv_im=np.hstack([m1ii, m1ir]),
        f2_re=np.hstack([f2r, -f2i]), f2_im=np.hstack([f2i, f2r]),
        f2conj_re=np.hstack([f2r, f2i]), f2conj_im=np.hstack([-f2i, f2r]),
    )
    return dict(n1=n1, n2=n2, mats={k: f(v) for k, v in mats.items()}, tw=(f(twr), f(twi)))


FFT_UNROLL = 4


def _fft_tools(n1, ct, twr_ref, twi_ref, a_ref):
    r = FFT_R
    reps = ct // V7X_LANES
    rows_out = n1 * r

    def rows(ref_slice_fn, count):
        return jnp.concatenate([ref_slice_fn(i) for i in range(count)], axis=0)

    def twiddle(tr, ti):
        if reps == 1:
            return tr, ti
        return jnp.concatenate([tr] * reps, axis=-1), jnp.concatenate([ti] * reps, axis=-1)

    def blk(j):
        return pl.ds(pl.multiple_of(j * r, r), r)

    def strided_fwd(mats, g, j):
        if len(mats) == 1:
            ar, ai = _dot(mats[0][:rows_out], g), _dot(mats[0][rows_out:], g)
        else:
            ar, ai = _dot(mats[0][...], g), _dot(mats[1][...], g)
        tr, ti = twiddle(rows(lambda i: twr_ref[i, blk(j), :], n1), rows(lambda i: twi_ref[i, blk(j), :], n1))
        br = (ar * tr - ai * ti).astype(BF16)
        bi = (ar * ti + ai * tr).astype(BF16)
        for i in range(n1):
            a_ref[0, i, blk(j), :] = br[i * r:(i + 1) * r]
            a_ref[1, i, blk(j), :] = bi[i * r:(i + 1) * r]

    return rows, twiddle, blk, strided_fwd


def _hyena_spectrum_kernel(hf_ref, hb_ref, m1_real_ref, f2_re_ref, f2_im_ref, twr_ref, twi_ref,
                           k_ref, a_ref, *, n1, n2, ct):
    h1 = n1 // 2
    rows, _, blk, strided_fwd = _fft_tools(n1, ct, twr_ref, twi_ref, a_ref)
    for part in range(2):
        def stage1(j, _):
            hf = rows(lambda i: hf_ref[i, blk(j), :], h1)
            hb = rows(lambda i: hb_ref[i, blk(j), :], h1)
            g = (hf + hb) if part == 0 else (hf - hb)
            strided_fwd((m1_real_ref,), g.astype(BF16), j)
            return 0
        lax.fori_loop(0, n2 // FFT_R, stage1, 0, unroll=FFT_UNROLL)

        def stage2(i, _):
            a = jnp.concatenate([a_ref[0, i], a_ref[1, i]], axis=0)
            f2 = f2_re_ref if part == 0 else f2_im_ref
            k_ref[part, i] = _dot(f2[...], a).astype(k_ref.dtype)
            return 0
        lax.fori_loop(0, n1, stage2, 0, unroll=FFT_UNROLL)


def _hyena_conv_kernel(u_ref, k_ref, d_ref, m1_re_ref, m1_im_ref, m1inv_re_ref, m1inv_im_ref,
                       f2_re_ref, f2_im_ref, f2conj_re_ref, f2conj_im_ref, twr_ref, twi_ref,
                       o_ref, a_ref, *, n1, n2, ct):
    r = FFT_R
    h1 = n1 // 2
    nblk = n2 // r
    rows, twiddle, blk, strided_fwd = _fft_tools(n1, ct, twr_ref, twi_ref, a_ref)

    def u_stage1(j, _):
        g = rows(lambda i: u_ref[i // h1, i % h1, blk(j), :], 2 * h1)
        strided_fwd((m1_re_ref, m1_im_ref), g, j)
        return 0
    lax.fori_loop(0, nblk, u_stage1, 0, unroll=FFT_UNROLL)

    def mid(i, _):
        a = jnp.concatenate([a_ref[0, i], a_ref[1, i]], axis=0)
        xr, xi = _dot(f2_re_ref[...], a), _dot(f2_im_ref[...], a)
        kr, ki = k_ref[0, i].astype(F32), k_ref[1, i].astype(F32)
        y = jnp.concatenate([(xr * kr - xi * ki).astype(BF16), (xr * ki + xi * kr).astype(BF16)], axis=0)
        br, bi = _dot(f2conj_re_ref[...], y), _dot(f2conj_im_ref[...], y)
        tr, ti = twiddle(twr_ref[i], twi_ref[i])
        a_ref[0, i] = (br * tr + bi * ti).astype(BF16)
        a_ref[1, i] = (bi * tr - br * ti).astype(BF16)
        return 0
    lax.fori_loop(0, n1, mid, 0, unroll=FFT_UNROLL)

    def last(j, _):
        bb = rows(lambda i: a_ref[i // n1, i % n1, blk(j), :], 2 * n1)
        y = (_dot(m1inv_re_ref[...], bb), _dot(m1inv_im_ref[...], bb))
        d = d_ref[...]
        for bidx in range(2):
            for i in range(h1):
                u = u_ref[bidx, i, blk(j), :].astype(F32)
                o_ref[bidx, i, blk(j), :] = (y[bidx][i * r:(i + 1) * r] + u * d).astype(o_ref.dtype)
        return 0
    lax.fori_loop(0, nblk, last, 0, unroll=FFT_UNROLL)


def _hyena_conv(u, filt, hy_d, ct):
    b, l, c = u.shape
    assert b == 2, "the two batch rows are packed as real / imaginary parts of one complex transform"
    consts = _fft_constants(l)
    n1, n2 = consts["n1"], consts["n2"]
    h1 = n1 // 2
    nct = c // ct
    u4 = u.reshape(b, h1, n2, c)
    f3 = filt.reshape(h1, n2, 2 * c)
    mats = {k: jnp.asarray(v, dtype=BF16) for k, v in consts["mats"].items()}
    spec_consts = [mats[k] for k in ("m1_real", "f2_re", "f2_im")]
    conv_consts = [mats[k] for k in ("m1_re", "m1_im", "m1inv_re", "m1inv_im",
                                     "f2_re", "f2_im", "f2conj_re", "f2conj_im")]
    tws = [jnp.asarray(a) for a in consts["tw"]]
    const_spec = lambda a: pl.BlockSpec(a.shape, lambda ci: (0,) * a.ndim, pipeline_mode=pl.Buffered(1))
    spectrum = pl.pallas_call(
        functools.partial(_hyena_spectrum_kernel, n1=n1, n2=n2, ct=ct),
        grid=(nct,),
        in_specs=[
            pl.BlockSpec((h1, n2, ct), lambda ci: (0, 0, ci)),
            pl.BlockSpec((h1, n2, ct), lambda ci: (0, 0, nct + ci)),
        ] + [const_spec(a) for a in (*spec_consts, *tws)],
        out_specs=pl.BlockSpec((2, n1, n2, ct), lambda ci: (0, 0, 0, ci)),
        out_shape=jax.ShapeDtypeStruct((2, n1, n2, c), BF16),
        scratch_shapes=[pltpu.VMEM((2, n1, n2, ct), BF16)],
        compiler_params=_params("arbitrary"),
        name="hyena_filter_spectrum",
    )(f3, f3, *spec_consts, *tws)
    out = pl.pallas_call(
        functools.partial(_hyena_conv_kernel, n1=n1, n2=n2, ct=ct),
        grid=(nct,),
        in_specs=[
            pl.BlockSpec((b, h1, n2, ct), lambda ci: (0, 0, 0, ci)),
            pl.BlockSpec((2, n1, n2, ct), lambda ci: (0, 0, 0, ci)),
            pl.BlockSpec((1, ct), lambda ci: (0, ci)),
        ] + [const_spec(a) for a in (*conv_consts, *tws)],
        out_specs=pl.BlockSpec((b, h1, n2, ct), lambda ci: (0, 0, 0, ci)),
        out_shape=jax.ShapeDtypeStruct((b, h1, n2, c), BF16),
        scratch_shapes=[pltpu.VMEM((2, n1, n2, ct), BF16)],
        compiler_params=_params("arbitrary"),
        name="hyena_long_conv",
    )(u4, spectrum, hy_d, *conv_consts, *tws)
    return out.reshape(b * l, c)


def _merge_kernel(att_ref, yc_ref, x0_ref, ga_ref, gh_ref, x_ref, wa_ref, wh_ref, wo_ref, g_ref, x1_ref, hf_ref):
    a = _dot(att_ref[...], wa_ref[...])
    yhy = x0_ref[...] * yc_ref[...].astype(F32)
    h = _dot(yhy.astype(BF16), wh_ref[...])
    merged = jax.nn.sigmoid(ga_ref[...].astype(F32)) * a + jax.nn.sigmoid(gh_ref[...].astype(F32)) * h
    x1 = x_ref[...] + _dot(merged.astype(BF16), wo_ref[...])
    x1_ref[...] = x1
    ms = jnp.mean(x1 * x1, axis=-1, keepdims=True)
    hf_ref[...] = (x1 * lax.rsqrt(ms + NORM_EPS) * g_ref[...]).astype(BF16)


def _merge(att, yc, x0, gates, x2, wa, wh, wo, g_ffn, tm):
    m, d = x2.shape
    row = lambda w: pl.BlockSpec((tm, w), lambda i: (i, 0))
    const = lambda a: pl.BlockSpec(a.shape, lambda i: (0, 0), pipeline_mode=pl.Buffered(1))
    return pl.pallas_call(
        _merge_kernel,
        grid=(m // tm,),
        in_specs=[row(att.shape[1]), row(yc.shape[1]), row(x0.shape[1]),
                  pl.BlockSpec((tm, d), lambda i: (i, 0)), pl.BlockSpec((tm, d), lambda i: (i, 1)),
                  row(d), const(wa), const(wh), const(wo), const(g_ffn)],
        out_specs=[row(d), row(d)],
        out_shape=[jax.ShapeDtypeStruct((m, d), F32), jax.ShapeDtypeStruct((m, d), BF16)],
        compiler_params=_params("arbitrary"),
        name="merge_out_proj",
    )(att, yc, x0, gates, gates, x2, wa, wh, wo, g_ffn)


FFN_HALO = V7X_BF16_SUBLANES


def _ffn_kernel(hf_ref, prev_ref, next_ref, x1_ref, wg_ref, wv_ref, cwg_ref, cwv_ref, cbg_ref, cbv_ref,
                wd_ref, g_ref, o_ref, ext_ref, acc_ref, *, tm, tiles_per_seq):
    i = pl.program_id(0)
    f = pl.program_id(1)
    hl = FFN_HALO

    @pl.when(f == 0)
    def _():
        first = (i % tiles_per_seq) == 0
        last = (i % tiles_per_seq) == tiles_per_seq - 1
        ext_ref[0:hl, :] = jnp.where(first, jnp.zeros_like(prev_ref[...]), prev_ref[...])
        ext_ref[hl:hl + tm, :] = hf_ref[...]
        ext_ref[hl + tm:, :] = jnp.where(last, jnp.zeros_like(next_ref[...]), next_ref[...])
        acc_ref[...] = jnp.zeros_like(acc_ref)

    ext = ext_ref[...]
    ne = tm + 2 * hl

    def conv(w_ref, cw_ref, cb_ref):
        up = _dot(ext, w_ref[...])
        cw = cw_ref[...]
        prev = pltpu.roll(up, 1, 0)
        nxt = pltpu.roll(up, ne - 1, 0)
        full = prev * cw[0:1, :] + up * cw[1:2, :] + nxt * cw[2:3, :] + cb_ref[...]
        return full[hl:hl + tm]

    gate = conv(wg_ref, cwg_ref, cbg_ref)
    val = conv(wv_ref, cwv_ref, cbv_ref)
    act = (gate * jax.nn.sigmoid(gate) * val).astype(BF16)
    acc_ref[...] += _dot(act, wd_ref[...])

    @pl.when(f == pl.num_programs(1) - 1)
    def _():
        x2 = x1_ref[...] + acc_ref[...]
        ms = jnp.mean(x2 * x2, axis=-1, keepdims=True)
        o_ref[...] = x2 * lax.rsqrt(ms + NORM_EPS) * g_ref[...]


def _ffn(hf, x1, w_up, conv_w, conv_b, w_down, g_final, l, tm, tf):
    m, d = hf.shape
    dff = w_down.shape[0]
    nf = dff // tf
    hl = FFN_HALO
    assert l % tm == 0 and tm % hl == 0 and dff % tf == 0
    hb = tm // hl
    nhb = m // hl
    return pl.pallas_call(
        functools.partial(_ffn_kernel, tm=tm, tiles_per_seq=l // tm),
        grid=(m // tm, nf),
        in_specs=[
            pl.BlockSpec((tm, d), lambda i, f: (i, 0)),
            pl.BlockSpec((hl, d), lambda i, f: (jnp.maximum(i * hb - 1, 0), 0)),
            pl.BlockSpec((hl, d), lambda i, f: (jnp.minimum((i + 1) * hb, nhb - 1), 0)),
            pl.BlockSpec((tm, d), lambda i, f: (i, 0)),
            pl.BlockSpec((d, tf), lambda i, f: (0, f)),
            pl.BlockSpec((d, tf), lambda i, f: (0, nf + f)),
            pl.BlockSpec((3, tf), lambda i, f: (0, f)),
            pl.BlockSpec((3, tf), lambda i, f: (0, nf + f)),
            pl.BlockSpec((1, tf), lambda i, f: (0, f)),
            pl.BlockSpec((1, tf), lambda i, f: (0, nf + f)),
            pl.BlockSpec((tf, d), lambda i, f: (f, 0)),
            pl.BlockSpec((1, d), lambda i, f: (0, 0)),
        ],
        out_specs=pl.BlockSpec((tm, d), lambda i, f: (i, 0)),
        out_shape=jax.ShapeDtypeStruct((m, d), F32),
        scratch_shapes=[pltpu.VMEM((tm + 2 * hl, d), BF16), pltpu.VMEM((tm, d), F32)],
        compiler_params=_params("arbitrary", "arbitrary"),
        name="conv_ffn",
    )(hf, hf, hf, x1, w_up, w_up, conv_w, conv_w, conv_b, conv_b, w_down, g_final)


def _tile(n, want):
    t = min(n, want)
    assert n % t == 0
    return t


def kernel(x, g_mix, w_in, lambda_q1, lambda_k1, lambda_q2, lambda_k2, g_subln, rel_bias, hy_conv_w,
           hy_conv_b, hy_f_w1, hy_f_b1, hy_f_w2, hy_f_b2, hy_f_w3, hy_f_b3, hy_f_w4, hy_freq, hy_d,
           w_attn_branch, w_hyena_branch, w_out, g_ffn, w_up, ffn_conv_w, ffn_conv_b, w_down, g_final):
    b, l, d = x.shape
    m = b * l
    depth = w_in.shape[0]
    assert depth == 1
    qkv_cols = 3 * ATTN_HEADS * HEAD_COLS
    width = hy_d.shape[1]
    hy_cols = 3 * width
    gate_cols = 2 * d
    assert w_in.shape[2] == qkv_cols + hy_cols + gate_cols

    x2 = x.reshape(m, d)
    row = lambda v: v.reshape(1, -1)

    filt, w_in_b = _hyena_filter(l, hy_f_w1[0], hy_f_b1, hy_f_w2[0], hy_f_b2, hy_f_w3[0], hy_f_b3, hy_f_w4[0],
                                 hy_freq, _tile(l, 512), w_in[0])

    qkv, hyp, gates = _norm_proj(x2, g_mix, w_in_b, ((qkv_cols, BF16), (hy_cols, BF16), (gate_cols, BF16)),
                                 _tile(m, 1024), math.gcd(qkv_cols, hy_cols, gate_cols, 1024))

    t_attn = _tile(l, 256)
    bias_tiles = _bias_tiles(rel_bias, t_attn)
    lam_vecs = jnp.concatenate([lambda_q1, lambda_k1, lambda_q2, lambda_k2], axis=0)
    att, (wa_b, wh_b, wo_b, w_up_b, w_down_b) = _attention(
        qkv.reshape(b, l, qkv_cols), bias_tiles, lam_vecs, g_subln, t_attn,
        (w_attn_branch[0], w_hyena_branch[0], w_out[0], w_up[0], w_down[0]))
    att = att.reshape(m, ATTN_HEADS * HEAD_COLS)

    x0, u = _hy_gate(hyp.reshape(b, l, hy_cols), hy_conv_w[0], hy_conv_b, _tile(width, V7X_LANES))
    yc = _hyena_conv(u, filt, hy_d, _tile(width, V7X_MXU_DIM))

    x1, hf = _merge(att, yc, x0.reshape(m, width), gates, x2, wa_b, wh_b, wo_b, g_ffn, _tile(m, 512))

    out = _ffn(hf, x1, w_up_b, ffn_conv_w[0], ffn_conv_b, w_down_b,
               row(g_final), l, _tile(l, 512), _tile(w_down.shape[1], 512))
    return out.reshape(b, l, d)
```

```python
import functools
import math

import numpy as np
import jax
import jax.numpy as jnp
from jax import lax
from jax.experimental import pallas as pl
from jax.experimental.pallas import tpu as pltpu

ATTN_HEADS = 8
ATTN_HEAD_DIM = 64
HEAD_COLS = 2 * ATTN_HEAD_DIM
N_BUCKETS = 32
MAX_DISTANCE = 128
NORM_EPS = 1e-6
SUBLN_EPS = 1e-5
LAMBDA_INIT = 0.8 - 0.6 * math.exp(-0.3 * 0)
FILTER_EMB_DIM = 33
FILTER_BANDS = (FILTER_EMB_DIM - 1) // 2
FAST_DECAY_PCT = 0.3
SLOW_DECAY_PCT = 1.5
DECAY_TARGET = 1e-2

V7X_LANES = 128
V7X_BF16_SUBLANES = 16
V7X_MXU_DIM = 256
V7X_VMEM_LIMIT_BYTES = 60 * 1024 * 1024

BF16 = jnp.bfloat16
F32 = jnp.float32


def _params(*sem):
    return pltpu.CompilerParams(dimension_semantics=sem, vmem_limit_bytes=V7X_VMEM_LIMIT_BYTES)


def _dot(a, b):
    return jnp.dot(a, b, preferred_element_type=F32)


def _dot_nt(a, b):
    return lax.dot_general(a, b, (((1,), (1,)), ((), ())), preferred_element_type=F32)


def _norm_proj_kernel(x_ref, g_ref, w_ref, *refs, bounds):
    out_refs, xn_ref = refs[:-1], refs[-1]
    j = pl.program_id(1)

    @pl.when(j == 0)
    def _():
        x = x_ref[...]
        ms = jnp.mean(x * x, axis=-1, keepdims=True)
        xn_ref[...] = (x * lax.rsqrt(ms + NORM_EPS) * g_ref[...]).astype(BF16)

    for o_ref, (lo, hi) in zip(out_refs, bounds):
        @pl.when((j >= lo) & (j < hi))
        def _(o_ref=o_ref):
            o_ref[...] = _dot(xn_ref[...], w_ref[...]).astype(o_ref.dtype)


def _norm_proj(x2, g, w, splits, tm, tn):
    m, d = x2.shape
    assert m % tm == 0 and all(nc % tn == 0 for nc, _ in splits)
    bounds, lo = [], 0
    for nc, _ in splits:
        bounds.append((lo, lo + nc // tn))
        lo += nc // tn
    assert lo * tn == w.shape[1]

    def out_spec(lo, hi):
        return pl.BlockSpec((tm, tn), lambda i, j: (i, jnp.clip(j - lo, 0, hi - lo - 1)))

    return pl.pallas_call(
        functools.partial(_norm_proj_kernel, bounds=tuple(bounds)),
        grid=(m // tm, lo),
        in_specs=[
            pl.BlockSpec((tm, d), lambda i, j: (i, 0)),
            pl.BlockSpec((1, d), lambda i, j: (0, 0)),
            pl.BlockSpec((d, tn), lambda i, j: (0, j)),
        ],
        out_specs=[out_spec(lo_, hi_) for lo_, hi_ in bounds],
        out_shape=[jax.ShapeDtypeStruct((m, nc), dt) for nc, dt in splits],
        scratch_shapes=[pltpu.VMEM((tm, d), BF16)],
        compiler_params=_params("arbitrary", "arbitrary"),
        name="norm_proj",
    )(x2, g, w)


def _t5_large_thresholds():
    half = N_BUCKETS // 2
    max_exact = half // 2
    n = np.arange(max_exact, 4 * MAX_DISTANCE, dtype=np.float64)
    large = max_exact + (np.log(n / max_exact) / math.log(MAX_DISTANCE / max_exact)
                         * (half - max_exact)).astype(np.int64)
    large = np.minimum(large, half - 1)
    thr = [int(n[np.argmax(large >= b)]) for b in range(max_exact + 1, half)]
    assert large[-1] == half - 1 and thr[-1] < MAX_DISTANCE
    return max_exact, half, thr


def _bias_tiles_kernel(rb_ref, o_ref, *, t):
    h = pl.program_id(0)
    max_exact, half, thr = _t5_large_thresholds()
    qi = lax.broadcasted_iota(jnp.int32, (t, t), 0)
    kj = lax.broadcasted_iota(jnp.int32, (t, t), 1)
    assert t + 1 >= thr[-1]
    for d in (0, 4):
        o_ref[d] = jnp.full((t, t), rb_ref[half - 1 + (half if d == 4 else 0), h], F32)
    for d in (1, 2, 3):
        rel = (d - 2) * t + kj - qi
        n = jnp.abs(rel)
        large = jnp.full((t, t), max_exact, jnp.int32)
        for th in thr:
            large = large + (n >= th).astype(jnp.int32)
        bucket = jnp.where(rel > 0, half, 0) + jnp.where(n < max_exact, n, large)
        tile = jnp.zeros((t, t), F32)
        for b in range(N_BUCKETS):
            tile = jnp.where(bucket == b, rb_ref[b, h], tile)
        o_ref[d] = tile


def _bias_tiles(rel_bias, t):
    assert t + 1 >= MAX_DISTANCE
    return pl.pallas_call(
        functools.partial(_bias_tiles_kernel, t=t),
        grid=(ATTN_HEADS,),
        in_specs=[pl.BlockSpec(memory_space=pltpu.SMEM)],
        out_specs=pl.BlockSpec((None, 5, t, t), lambda h: (h, 0, 0, 0)),
        out_shape=jax.ShapeDtypeStruct((ATTN_HEADS, 5, t, t), F32),
        compiler_params=_params("arbitrary"),
        name="t5_bias_tiles",
    )(rel_bias)


def _attn_kernel(q_ref, k_ref, v_ref, bias_ref, lam_ref, g_ref, *refs, t, nkb, nq, n_groups, per, n_cast):
    cast_in, o_ref, cast_out = refs[:n_cast], refs[n_cast], refs[n_cast + 1:2 * n_cast + 1]
    sa_ref, sb_ref, ma_ref, mb_ref, vext_ref = refs[2 * n_cast + 1:]
    step = pl.program_id(0)
    groups_per_head = nq // per
    qg = jnp.minimum(step, n_groups - 1) % groups_per_head
    done_group = jnp.maximum(step - 1, 0)

    @pl.when(step == 0)
    def _():
        sb_ref[...] = jnp.zeros_like(sb_ref)
        mb_ref[...] = jnp.zeros_like(mb_ref)

    @pl.when(done_group % groups_per_head == 0)
    def _():
        vext_ref[:, :HEAD_COLS] = v_ref[...]
        vext_ref[:, HEAD_COLS:] = jnp.ones((vext_ref.shape[0], HEAD_COLS), BF16)

    def body(cur_s, cur_m, prev_s, prev_m):
        for w_ref, wb_ref in zip(cast_in, cast_out):
            wb_ref[...] = w_ref[...].astype(BF16)
        lq1, lk1, lq2, lk2 = (lam_ref[i:i + 1, :] for i in range(4))
        lam = (jnp.exp(jnp.sum(lq1 * lk1, keepdims=True)) - jnp.exp(jnp.sum(lq2 * lk2, keepdims=True))
               + LAMBDA_INIT)

        def score(ii):
            q = q_ref[ii * t:(ii + 1) * t, :] * (ATTN_HEAD_DIM ** -0.5)
            lane = lax.broadcasted_iota(jnp.int32, q.shape, 1)
            qs = (jnp.where(lane < ATTN_HEAD_DIM, q, 0).astype(BF16),
                  jnp.where(lane >= ATTN_HEAD_DIM, q, 0).astype(BF16))
            qb = qg * per + ii
            m = [jnp.full((t, 1), -jnp.inf, F32) for _ in range(2)]
            for kb in range(nkb):
                kblk = k_ref[kb * t:(kb + 1) * t, :]
                bias = bias_ref[jnp.clip(kb - qb, -2, 2) + 2]
                for j in range(2):
                    s = _dot_nt(qs[j], kblk) + bias
                    cur_s[ii, j, :, kb * t:(kb + 1) * t] = s
                    m[j] = jnp.maximum(m[j], jnp.max(s, axis=-1, keepdims=True))
            for j in range(2):
                cur_m[ii, j] = m[j]

        def finish(ii):
            outs = []
            for j in range(2):
                mj = prev_m[ii, j]
                acc = jnp.zeros((t, 2 * HEAD_COLS), F32)
                for kb in range(nkb):
                    p = jnp.exp(prev_s[ii, j, :, kb * t:(kb + 1) * t] - mj).astype(BF16)
                    acc = acc + _dot(p, vext_ref[kb * t:(kb + 1) * t, :])
                outs.append(acc[:, :HEAD_COLS] / acc[:, HEAD_COLS:])
            att = outs[0] - lam * outs[1]
            ms = jnp.mean(att * att, axis=-1, keepdims=True)
            att = att * lax.rsqrt(ms + SUBLN_EPS) * g_ref[...] * (1.0 - LAMBDA_INIT)
            o_ref[ii * t:(ii + 1) * t, :] = att.astype(o_ref.dtype)

        for ii in range(per):
            score(ii)
            finish(ii)

    @pl.when(step % 2 == 0)
    def _():
        body(sa_ref, ma_ref, sb_ref, mb_ref)

    @pl.when(step % 2 == 1)
    def _():
        body(sb_ref, mb_ref, sa_ref, ma_ref)


def _cast_chunks(w, n_steps):
    rows, cols = w.shape
    per = V7X_BF16_SUBLANES
    while rows % per or rows // per > n_steps:
        per += V7X_BF16_SUBLANES
        assert per <= rows
    ncb = 1
    while (rows // per) * ncb * 2 <= n_steps and cols % (ncb * 2 * V7X_LANES) == 0:
        ncb *= 2
    return w.reshape(rows // per, per, cols), ncb


def _attention(qkv3, bias_tiles, lam_vecs, g_subln, t, cast_weights):
    b, l, _ = qkv3.shape
    nkb = nq = l // t
    per = 2 if nq % 2 == 0 else 1
    gph = nq // per
    n_groups = b * ATTN_HEADS * gph
    kcol0 = ATTN_HEADS
    vcol0 = 2 * ATTN_HEADS
    chunked = [_cast_chunks(w, n_groups + 1) for w in cast_weights]

    def cast_spec(w3, ncb):
        n = w3.shape[0] * ncb

        def index(s):
            c = jnp.minimum(s, n - 1)
            return c // ncb, 0, c % ncb

        return pl.BlockSpec((None, w3.shape[1], w3.shape[2] // ncb), index)

    def group(s):
        return s // (ATTN_HEADS * gph), (s // gph) % ATTN_HEADS, s % gph

    def scored(s):
        return group(jnp.minimum(s, n_groups - 1))

    def finished(s):
        return group(jnp.maximum(s - 1, 0))

    outs = pl.pallas_call(
        functools.partial(_attn_kernel, t=t, nkb=nkb, nq=nq, n_groups=n_groups, per=per,
                          n_cast=len(chunked)),
        grid=(n_groups + 1,),
        in_specs=[
            pl.BlockSpec((None, per * t, HEAD_COLS), lambda s: (scored(s)[0], scored(s)[2], scored(s)[1])),
            pl.BlockSpec((None, l, HEAD_COLS), lambda s: (scored(s)[0], 0, kcol0 + scored(s)[1])),
            pl.BlockSpec((None, l, HEAD_COLS), lambda s: (finished(s)[0], 0, vcol0 + finished(s)[1])),
            pl.BlockSpec((None, 5, t, t), lambda s: (scored(s)[1], 0, 0, 0)),
            pl.BlockSpec((4, ATTN_HEAD_DIM), lambda s: (0, 0)),
            pl.BlockSpec((1, HEAD_COLS), lambda s: (0, 0)),
        ] + [cast_spec(w3, ncb) for w3, ncb in chunked],
        out_specs=[pl.BlockSpec((None, per * t, HEAD_COLS),
                                lambda s: (finished(s)[0], finished(s)[2], finished(s)[1]))]
        + [cast_spec(w3, ncb) for w3, ncb in chunked],
        out_shape=[jax.ShapeDtypeStruct((b, l, ATTN_HEADS * HEAD_COLS), BF16)]
        + [jax.ShapeDtypeStruct(w3.shape, BF16) for w3, _ in chunked],
        scratch_shapes=[pltpu.VMEM((per, 2, t, l), F32), pltpu.VMEM((per, 2, t, l), F32),
                        pltpu.VMEM((per, 2, t, 1), F32), pltpu.VMEM((per, 2, t, 1), F32),
                        pltpu.VMEM((l, 2 * HEAD_COLS), BF16)],
        compiler_params=_params("arbitrary"),
        name="diff_attention",
    )(qkv3, qkv3, qkv3, bias_tiles, lam_vecs, g_subln, *[w3 for w3, _ in chunked])
    return outs[0], [wb.reshape(w.shape) for wb, w in zip(outs[1:], cast_weights)]


def _filter_kernel(w1t_ref, w1c_ref, w1s_ref, b1_ref, w2_ref, b2_ref, w3_ref, b3_ref, w4_ref, fr_ref,
                   cast_ref, o_ref, cast_out_ref, *, l, tl, width):
    cast_out_ref[...] = cast_ref[...].astype(BF16)
    i = pl.program_id(0)
    pos_row = (lax.broadcasted_iota(jnp.int32, (1, tl), 1) + i * tl).astype(F32)
    ang = 2.0 * math.pi * pos_row / l
    bi = lax.broadcasted_iota(jnp.int32, (FILTER_BANDS, 1), 0).astype(F32)
    bands = 1e-4 + bi * ((FILTER_BANDS - 1 - 1e-4) / (FILTER_BANDS - 1))
    pre = (w1t_ref[...] * (pos_row / (l - 1)) + _dot(w1c_ref[...], jnp.cos(bands * ang))
           - _dot(w1s_ref[...], jnp.sin(bands * ang)))
    fr = fr_ref[...]
    h = jnp.sin(fr * (pre + b1_ref[...]))
    h = jnp.sin(fr * (_dot(w2_ref[...], h) + b2_ref[...]))
    h = jnp.sin(fr * (_dot(w3_ref[...], h) + b3_ref[...]))
    h = lax.dot_general(h, w4_ref[...], (((0,), (0,)), ((), ())), preferred_element_type=F32)
    pos = (lax.broadcasted_iota(jnp.int32, (tl, 1), 0) + i * tl).astype(F32)
    t = pos / (l - 1)
    max_decay = math.log(DECAY_TARGET) / FAST_DECAY_PCT
    min_decay = math.log(DECAY_TARGET) / SLOW_DECAY_PCT
    ci = lax.broadcasted_iota(jnp.int32, (1, 2 * width), 1)
    ci = jnp.where(ci >= width, ci - width, ci).astype(F32)
    deltas = jnp.abs(min_decay + ci * ((max_decay - min_decay) / (width - 1)))
    o_ref[...] = h * jnp.exp(-t * deltas)


def _hyena_filter(l, w1, b1, w2, b2, w3, b3, w4, freq, tl, cast_weight):
    width = w4.shape[1] // 2
    steps = l // tl
    rows, cols = cast_weight.shape
    assert rows % (steps * V7X_BF16_SUBLANES) == 0
    cast_spec = pl.BlockSpec((rows // steps, cols), lambda i: (i, 0))
    full = lambda a: pl.BlockSpec(a.shape, lambda i: (0,) * a.ndim)
    args = (w1[0:1].T, w1[1:1 + FILTER_BANDS].T, w1[1 + FILTER_BANDS:].T, b1.T, w2.T, b2.T, w3.T, b3.T,
            w4, freq.T)
    return pl.pallas_call(
        functools.partial(_filter_kernel, l=l, tl=tl, width=width),
        grid=(steps,),
        in_specs=[full(a) for a in args] + [cast_spec],
        out_specs=[pl.BlockSpec((tl, 2 * width), lambda i: (i, 0)), cast_spec],
        out_shape=[jax.ShapeDtypeStruct((l, 2 * width), F32), jax.ShapeDtypeStruct((rows, cols), BF16)],
        compiler_params=_params("arbitrary"),
        name="hyena_filter",
    )(*args, cast_weight)


def _shifted_rows(z):
    n = z.shape[0]
    g = 8
    row = lax.broadcasted_iota(jnp.int32, (g, z.shape[1]), 0)
    prev = pltpu.roll(z, 1, 0)
    nxt = pltpu.roll(z, n - 1, 0)
    prev = jnp.concatenate([jnp.where(row == 0, 0.0, prev[:g]), prev[g:]], axis=0)
    nxt = jnp.concatenate([nxt[:n - g], jnp.where(row == g - 1, 0.0, nxt[n - g:])], axis=0)
    return prev, nxt


def _dwconv_rows(z, w, b):
    prev, nxt = _shifted_rows(z)
    return prev * w[0:1, :] + z * w[1:2, :] + nxt * w[2:3, :] + b


def _hy_gate_kernel(a0, a1, a2, w0, w1, w2, b0, b1, b2, x0_ref, u_ref):
    x0_ref[...] = _dwconv_rows(a0[...].astype(F32), w0[...], b0[...])
    u = (_dwconv_rows(a1[...].astype(F32), w1[...], b1[...])
         * _dwconv_rows(a2[...].astype(F32), w2[...], b2[...]))
    u_ref[...] = u.astype(u_ref.dtype)


def _hy_gate(hy3, conv_w, conv_b, ct):
    b, l, cols = hy3.shape
    c = cols // 3
    nct = c // ct
    a_spec = lambda k: pl.BlockSpec((None, l, ct), lambda bi, ci: (bi, 0, k * nct + ci))
    w_spec = lambda k: pl.BlockSpec((3, ct), lambda bi, ci: (0, k * nct + ci))
    b_spec = lambda k: pl.BlockSpec((1, ct), lambda bi, ci: (0, k * nct + ci))
    o_spec = pl.BlockSpec((None, l, ct), lambda bi, ci: (bi, 0, ci))
    return pl.pallas_call(
        _hy_gate_kernel,
        grid=(b, nct),
        in_specs=[a_spec(0), a_spec(1), a_spec(2), w_spec(0), w_spec(1), w_spec(2),
                  b_spec(0), b_spec(1), b_spec(2)],
        out_specs=[o_spec, o_spec],
        out_shape=[jax.ShapeDtypeStruct((b, l, c), F32), jax.ShapeDtypeStruct((b, l, c), BF16)],
        compiler_params=_params("arbitrary", "arbitrary"),
        name="hyena_gate",
    )(hy3, hy3, hy3, conv_w, conv_w, conv_w, conv_b, conv_b, conv_b)


FFT_N2 = V7X_MXU_DIM
FFT_R = V7X_BF16_SUBLANES


@functools.lru_cache(maxsize=None)
def _fft_constants(l):
    n = 2 * l
    n2 = FFT_N2
    n1 = n // n2
    assert n1 * n2 == n and n1 % 2 == 0
    r = FFT_R
    eye = np.eye(r)
    k1 = np.arange(n1)[:, None]
    a1 = 2.0 * np.pi * k1 * np.arange(n1 // 2)[None, :] / n1
    m1r = np.kron(np.cos(a1), eye)
    m1i = np.kron(-np.sin(a1), eye)
    m1ir = np.kron(np.cos(a1).T, eye) / n
    m1ii = np.kron(np.sin(a1).T, eye) / n
    a2 = 2.0 * np.pi * np.arange(n2)[:, None] * np.arange(n2)[None, :] / n2
    f2r, f2i = np.cos(a2), -np.sin(a2)
    at = 2.0 * np.pi * k1 * np.arange(n2)[None, :] / n
    twr = np.broadcast_to(np.cos(at)[:, :, None], (n1, n2, V7X_LANES))
    twi = np.broadcast_to(-np.sin(at)[:, :, None], (n1, n2, V7X_LANES))
    f = lambda a: np.ascontiguousarray(a, dtype=np.float32)
    mats = dict(
        m1_real=np.vstack([m1r, m1i]),
        m1_re=np.hstack([m1r, -m1i]), m1_im=np.hstack([m1i, m1r]),
        m1inv_re=np.hstack([m1ir, -m1ii]), m1inv_im=np.hstack([m1ii, m1ir]),
        f2_re=np.hstack([f2r, -f2i]), f2_im=np.hstack([f2i, f2r]),
        f2conj_re=np.hstack([f2r, f2i]), f2conj_im=np.hstack([-f2i, f2r]),
    )
    return dict(n1=n1, n2=n2, mats={k: f(v) for k, v in mats.items()}, tw=(f(twr), f(twi)))


FFT_UNROLL = 4


def _fft_tools(n1, ct, twr_ref, twi_ref, a_ref):
    r = FFT_R
    reps = ct // V7X_LANES
    rows_out = n1 * r

    def rows(ref_slice_fn, count):
        return jnp.concatenate([ref_slice_fn(i) for i in range(count)], axis=0)

    def twiddle(tr, ti):
        if reps == 1:
            return tr, ti
        return jnp.concatenate([tr] * reps, axis=-1), jnp.concatenate([ti] * reps, axis=-1)

    def blk(j):
        return pl.ds(pl.multiple_of(j * r, r), r)

    def strided_fwd(mats, g, j):
        if len(mats) == 1:
            ar, ai = _dot(mats[0][:rows_out], g), _dot(mats[0][rows_out:], g)
        else:
            ar, ai = _dot(mats[0][...], g), _dot(mats[1][...], g)
        tr, ti = twiddle(rows(lambda i: twr_ref[i, blk(j), :], n1), rows(lambda i: twi_ref[i, blk(j), :], n1))
        br = (ar * tr - ai * ti).astype(BF16)
        bi = (ar * ti + ai * tr).astype(BF16)
        for i in range(n1):
            a_ref[0, i, blk(j), :] = br[i * r:(i + 1) * r]
            a_ref[1, i, blk(j), :] = bi[i * r:(i + 1) * r]

    return rows, twiddle, blk, strided_fwd


def _hyena_spectrum_kernel(hf_ref, hb_ref, m1_real_ref, f2_re_ref, f2_im_ref, twr_ref, twi_ref,
                           k_ref, a_ref, *, n1, n2, ct):
    h1 = n1 // 2
    rows, _, blk, strided_fwd = _fft_tools(n1, ct, twr_ref, twi_ref, a_ref)
    for part in range(2):
        def stage1(j, _):
            hf = rows(lambda i: hf_ref[i, blk(j), :], h1)
            hb = rows(lambda i: hb_ref[i, blk(j), :], h1)
            g = (hf + hb) if part == 0 else (hf - hb)
            strided_fwd((m1_real_ref,), g.astype(BF16), j)
            return 0
        lax.fori_loop(0, n2 // FFT_R, stage1, 0, unroll=FFT_UNROLL)

        def stage2(i, _):
            a = jnp.concatenate([a_ref[0, i], a_ref[1, i]], axis=0)
            f2 = f2_re_ref if part == 0 else f2_im_ref
            k_ref[part, i] = _dot(f2[...], a).astype(k_ref.dtype)
            return 0
        lax.fori_loop(0, n1, stage2, 0, unroll=FFT_UNROLL)


def _hyena_conv_kernel(u_ref, k_ref, d_ref, m1_re_ref, m1_im_ref, m1inv_re_ref, m1inv_im_ref,
                       f2_re_ref, f2_im_ref, f2conj_re_ref, f2conj_im_ref, twr_ref, twi_ref,
                       o_ref, a_ref, *, n1, n2, ct):
    r = FFT_R
    h1 = n1 // 2
    nblk = n2 // r
    rows, twiddle, blk, strided_fwd = _fft_tools(n1, ct, twr_ref, twi_ref, a_ref)

    def u_stage1(j, _):
        g = rows(lambda i: u_ref[i // h1, i % h1, blk(j), :], 2 * h1)
        strided_fwd((m1_re_ref, m1_im_ref), g, j)
        return 0
    lax.fori_loop(0, nblk, u_stage1, 0, unroll=FFT_UNROLL)

    def mid(i, _):
        a = jnp.concatenate([a_ref[0, i], a_ref[1, i]], axis=0)
        xr, xi = _dot(f2_re_ref[...], a), _dot(f2_im_ref[...], a)
        kr, ki = k_ref[0, i].astype(F32), k_ref[1, i].astype(F32)
        y = jnp.concatenate([(xr * kr - xi * ki).astype(BF16), (xr * ki + xi * kr).astype(BF16)], axis=0)
        br, bi = _dot(f2conj_re_ref[...], y), _dot(f2conj_im_ref[...], y)
        tr, ti = twiddle(twr_ref[i], twi_ref[i])
        a_ref[0, i] = (br * tr + bi * ti).astype(BF16)
        a_ref[1, i] = (bi * tr - br * ti).astype(BF16)
        return 0
    lax.fori_loop(0, n1, mid, 0, unroll=FFT_UNROLL)

    def last(j, _):
        bb = rows(lambda i: a_ref[i // n1, i % n1, blk(j), :], 2 * n1)
        y = (_dot(m1inv_re_ref[...], bb), _dot(m1inv_im_ref[...], bb))
        d = d_ref[...]
        for bidx in range(2):
            for i in range(h1):
                u = u_ref[bidx, i, blk(j), :].astype(F32)
                o_ref[bidx, i, blk(j), :] = (y[bidx][i * r:(i + 1) * r] + u * d).astype(o_ref.dtype)
        return 0
    lax.fori_loop(0, nblk, last, 0, unroll=FFT_UNROLL)


def _hyena_conv(u, filt, hy_d, ct):
    b, l, c = u.shape
    assert b == 2, "the two batch rows are packed as real / imaginary parts of one complex transform"
    consts = _fft_constants(l)
    n1, n2 = consts["n1"], consts["n2"]
    h1 = n1 // 2
    nct = c // ct
    u4 = u.reshape(b, h1, n2, c)
    f3 = filt.reshape(h1, n2, 2 * c)
    mats = {k: jnp.asarray(v, dtype=BF16) for k, v in consts["mats"].items()}
    spec_consts = [mats[k] for k in ("m1_real", "f2_re", "f2_im")]
    conv_consts = [mats[k] for k in ("m1_re", "m1_im", "m1inv_re", "m1inv_im",
                                     "f2_re", "f2_im", "f2conj_re", "f2conj_im")]
    tws = [jnp.asarray(a) for a in consts["tw"]]
    const_spec = lambda a: pl.BlockSpec(a.shape, lambda ci: (0,) * a.ndim, pipeline_mode=pl.Buffered(1))
    spectrum = pl.pallas_call(
        functools.partial(_hyena_spectrum_kernel, n1=n1, n2=n2, ct=ct),
        grid=(nct,),
        in_specs=[
            pl.BlockSpec((h1, n2, ct), lambda ci: (0, 0, ci)),
            pl.BlockSpec((h1, n2, ct), lambda ci: (0, 0, nct + ci)),
        ] + [const_spec(a) for a in (*spec_consts, *tws)],
        out_specs=pl.BlockSpec((2, n1, n2, ct), lambda ci: (0, 0, 0, ci)),
        out_shape=jax.ShapeDtypeStruct((2, n1, n2, c), BF16),
        scratch_shapes=[pltpu.VMEM((2, n1, n2, ct), BF16)],
        compiler_params=_params("arbitrary"),
        name="hyena_filter_spectrum",
    )(f3, f3, *spec_consts, *tws)
    out = pl.pallas_call(
        functools.partial(_hyena_conv_kernel, n1=n1, n2=n2, ct=ct),
        grid=(nct,),
        in_specs=[
            pl.BlockSpec((b, h1, n2, ct), lambda ci: (0, 0, 0, ci)),
            pl.BlockSpec((2, n1, n2, ct), lambda ci: (0, 0, 0, ci)),
            pl.BlockSpec((1, ct), lambda ci: (0, ci)),
        ] + [const_spec(a) for a in (*conv_consts, *tws)],
        out_specs=pl.BlockSpec((b, h1, n2, ct), lambda ci: (0, 0, 0, ci)),
        out_shape=jax.ShapeDtypeStruct((b, h1, n2, c), BF16),
        scratch_shapes=[pltpu.VMEM((2, n1, n2, ct), BF16)],
        compiler_params=_params("arbitrary"),
        name="hyena_long_conv",
    )(u4, spectrum, hy_d, *conv_consts, *tws)
    return out.reshape(b * l, c)


def _merge_kernel(att_ref, yc_ref, x0_ref, ga_ref, gh_ref, x_ref, wa_ref, wh_ref, wo_ref, g_ref, x1_ref, hf_ref):
    a = _dot(att_ref[...], wa_ref[...])
    yhy = x0_ref[...] * yc_ref[...].astype(F32)
    h = _dot(yhy.astype(BF16), wh_ref[...])
    merged = jax.nn.sigmoid(ga_ref[...].astype(F32)) * a + jax.nn.sigmoid(gh_ref[...].astype(F32)) * h
    x1 = x_ref[...] + _dot(merged.astype(BF16), wo_ref[...])
    x1_ref[...] = x1
    ms = jnp.mean(x1 * x1, axis=-1, keepdims=True)
    hf_ref[...] = (x1 * lax.rsqrt(ms + NORM_EPS) * g_ref[...]).astype(BF16)


def _merge(att, yc, x0, gates, x2, wa, wh, wo, g_ffn, tm):
    m, d = x2.shape
    row = lambda w: pl.BlockSpec((tm, w), lambda i: (i, 0))
    const = lambda a: pl.BlockSpec(a.shape, lambda i: (0, 0), pipeline_mode=pl.Buffered(1))
    return pl.pallas_call(
        _merge_kernel,
        grid=(m // tm,),
        in_specs=[row(att.shape[1]), row(yc.shape[1]), row(x0.shape[1]),
                  pl.BlockSpec((tm, d), lambda i: (i, 0)), pl.BlockSpec((tm, d), lambda i: (i, 1)),
                  row(d), const(wa), const(wh), const(wo), const(g_ffn)],
        out_specs=[row(d), row(d)],
        out_shape=[jax.ShapeDtypeStruct((m, d), F32), jax.ShapeDtypeStruct((m, d), BF16)],
        compiler_params=_params("arbitrary"),
        name="merge_out_proj",
    )(att, yc, x0, gates, gates, x2, wa, wh, wo, g_ffn)


FFN_HALO = V7X_BF16_SUBLANES


def _ffn_kernel(hf_ref, prev_ref, next_ref, x1_ref, wg_ref, wv_ref, cwg_ref, cwv_ref, cbg_ref, cbv_ref,
                wd_ref, g_ref, o_ref, ext_ref, acc_ref, *, tm, tiles_per_seq):
    i = pl.program_id(0)
    f = pl.program_id(1)
    hl = FFN_HALO

    @pl.when(f == 0)
    def _():
        first = (i % tiles_per_seq) == 0
        last = (i % tiles_per_seq) == tiles_per_seq - 1
        ext_ref[0:hl, :] = jnp.where(first, jnp.zeros_like(prev_ref[...]), prev_ref[...])
        ext_ref[hl:hl + tm, :] = hf_ref[...]
        ext_ref[hl + tm:, :] = jnp.where(last, jnp.zeros_like(next_ref[...]), next_ref[...])
        acc_ref[...] = jnp.zeros_like(acc_ref)

    ext = ext_ref[...]
    ne = tm + 2 * hl

    def conv(w_ref, cw_ref, cb_ref):
        up = _dot(ext, w_ref[...])
        cw = cw_ref[...]
        prev = pltpu.roll(up, 1, 0)
        nxt = pltpu.roll(up, ne - 1, 0)
        full = prev * cw[0:1, :] + up * cw[1:2, :] + nxt * cw[2:3, :] + cb_ref[...]
        return full[hl:hl + tm]

    gate = conv(wg_ref, cwg_ref, cbg_ref)
    val = conv(wv_ref, cwv_ref, cbv_ref)
    act = (gate * jax.nn.sigmoid(gate) * val).astype(BF16)
    acc_ref[...] += _dot(act, wd_ref[...])

    @pl.when(f == pl.num_programs(1) - 1)
    def _():
        x2 = x1_ref[...] + acc_ref[...]
        ms = jnp.mean(x2 * x2, axis=-1, keepdims=True)
        o_ref[...] = x2 * lax.rsqrt(ms + NORM_EPS) * g_ref[...]


def _ffn(hf, x1, w_up, conv_w, conv_b, w_down, g_final, l, tm, tf):
    m, d = hf.shape
    dff = w_down.shape[0]
    nf = dff // tf
    hl = FFN_HALO
    assert l % tm == 0 and tm % hl == 0 and dff % tf == 0
    hb = tm // hl
    nhb = m // hl
    return pl.pallas_call(
        functools.partial(_ffn_kernel, tm=tm, tiles_per_seq=l // tm),
        grid=(m // tm, nf),
        in_specs=[
            pl.BlockSpec((tm, d), lambda i, f: (i, 0)),
            pl.BlockSpec((hl, d), lambda i, f: (jnp.maximum(i * hb - 1, 0), 0)),
            pl.BlockSpec((hl, d), lambda i, f: (jnp.minimum((i + 1) * hb, nhb - 1), 0)),
            pl.BlockSpec((tm, d), lambda i, f: (i, 0)),
            pl.BlockSpec((d, tf), lambda i, f: (0, f)),
            pl.BlockSpec((d, tf), lambda i, f: (0, nf + f)),
            pl.BlockSpec((3, tf), lambda i, f: (0, f)),
            pl.BlockSpec((3, tf), lambda i, f: (0, nf + f)),
            pl.BlockSpec((1, tf), lambda i, f: (0, f)),
            pl.BlockSpec((1, tf), lambda i, f: (0, nf + f)),
            pl.BlockSpec((tf, d), lambda i, f: (f, 0)),
            pl.BlockSpec((1, d), lambda i, f: (0, 0)),
        ],
        out_specs=pl.BlockSpec((tm, d), lambda i, f: (i, 0)),
        out_shape=jax.ShapeDtypeStruct((m, d), F32),
        scratch_shapes=[pltpu.VMEM((tm + 2 * hl, d), BF16), pltpu.VMEM((tm, d), F32)],
        compiler_params=_params("arbitrary", "arbitrary"),
        name="conv_ffn",
    )(hf, hf, hf, x1, w_up, w_up, conv_w, conv_w, conv_b, conv_b, w_down, g_final)


def _tile(n, want):
    t = min(n, want)
    assert n % t == 0
    return t


def kernel(x, g_mix, w_in, lambda_q1, lambda_k1, lambda_q2, lambda_k2, g_subln, rel_bias, hy_conv_w,
           hy_conv_b, hy_f_w1, hy_f_b1, hy_f_w2, hy_f_b2, hy_f_w3, hy_f_b3, hy_f_w4, hy_freq, hy_d,
           w_attn_branch, w_hyena_branch, w_out, g_ffn, w_up, ffn_conv_w, ffn_conv_b, w_down, g_final):
    b, l, d = x.shape
    m = b * l
    depth = w_in.shape[0]
    assert depth == 1
    qkv_cols = 3 * ATTN_HEADS * HEAD_COLS
    width = hy_d.shape[1]
    hy_cols = 3 * width
    gate_cols = 2 * d
    assert w_in.shape[2] == qkv_cols + hy_cols + gate_cols

    x2 = x.reshape(m, d)
    row = lambda v: v.reshape(1, -1)

    filt, w_in_b = _hyena_filter(l, hy_f_w1[0], hy_f_b1, hy_f_w2[0], hy_f_b2, hy_f_w3[0], hy_f_b3, hy_f_w4[0],
                                 hy_freq, _tile(l, 512), w_in[0])

    qkv, hyp, gates = _norm_proj(x2, g_mix, w_in_b, ((qkv_cols, BF16), (hy_cols, BF16), (gate_cols, BF16)),
                                 _tile(m, 1024), math.gcd(qkv_cols, hy_cols, gate_cols, 1024))

    t_attn = _tile(l, 256)
    bias_tiles = _bias_tiles(rel_bias, t_attn)
    lam_vecs = jnp.concatenate([lambda_q1, lambda_k1, lambda_q2, lambda_k2], axis=0)
    att, (wa_b, wh_b, wo_b, w_up_b, w_down_b) = _attention(
        qkv.reshape(b, l, qkv_cols), bias_tiles, lam_vecs, g_subln, t_attn,
        (w_attn_branch[0], w_hyena_branch[0], w_out[0], w_up[0], w_down[0]))
    att = att.reshape(m, ATTN_HEADS * HEAD_COLS)

    x0, u = _hy_gate(hyp.reshape(b, l, hy_cols), hy_conv_w[0], hy_conv_b, _tile(width, V7X_LANES))
    yc = _hyena_conv(u, filt, hy_d, _tile(width, V7X_MXU_DIM))

    x1, hf = _merge(att, yc, x0.reshape(m, width), gates, x2, wa_b, wh_b, wo_b, g_ffn, _tile(m, 512))

    out = _ffn(hf, x1, w_up_b, ffn_conv_w[0], ffn_conv_b, w_down_b,
               row(g_final), l, _tile(l, 512), _tile(w_down.shape[1], 512))
    return out.reshape(b, l, d)
```

```python
import functools
import math

import numpy as np
import jax
import jax.numpy as jnp
from jax import lax
from jax.experimental import pallas as pl
from jax.experimental.pallas import tpu as pltpu

ATTN_HEADS = 8
ATTN_HEAD_DIM = 64
HEAD_COLS = 2 * ATTN_HEAD_DIM
N_BUCKETS = 32
MAX_DISTANCE = 128
NORM_EPS = 1e-6
SUBLN_EPS = 1e-5
LAMBDA_INIT = 0.8 - 0.6 * math.exp(-0.3 * 0)
FILTER_EMB_DIM = 33
FILTER_BANDS = (FILTER_EMB_DIM - 1) // 2
FAST_DECAY_PCT = 0.3
SLOW_DECAY_PCT = 1.5
DECAY_TARGET = 1e-2

V7X_LANES = 128
V7X_BF16_SUBLANES = 16
V7X_MXU_DIM = 256
V7X_VMEM_LIMIT_BYTES = 60 * 1024 * 1024

BF16 = jnp.bfloat16
F32 = jnp.float32


def _params(*sem):
    return pltpu.CompilerParams(dimension_semantics=sem, vmem_limit_bytes=V7X_VMEM_LIMIT_BYTES)


def _dot(a, b):
    return jnp.dot(a, b, preferred_element_type=F32)


def _dot_nt(a, b):
    return lax.dot_general(a, b, (((1,), (1,)), ((), ())), preferred_element_type=F32)


def _norm_proj_kernel(x_ref, g_ref, w_ref, *refs, bounds):
    out_refs, xn_ref = refs[:-1], refs[-1]
    j = pl.program_id(1)

    @pl.when(j == 0)
    def _():
        x = x_ref[...]
        ms = jnp.mean(x * x, axis=-1, keepdims=True)
        xn_ref[...] = (x * lax.rsqrt(ms + NORM_EPS) * g_ref[...]).astype(BF16)

    for o_ref, (lo, hi) in zip(out_refs, bounds):
        @pl.when((j >= lo) & (j < hi))
        def _(o_ref=o_ref):
            o_ref[...] = _dot(xn_ref[...], w_ref[...]).astype(o_ref.dtype)


def _norm_proj(x2, g, w, splits, tm, tn):
    m, d = x2.shape
    assert m % tm == 0 and all(nc % tn == 0 for nc, _ in splits)
    bounds, lo = [], 0
    for nc, _ in splits:
        bounds.append((lo, lo + nc // tn))
        lo += nc // tn
    assert lo * tn == w.shape[1]

    def out_spec(lo, hi):
        return pl.BlockSpec((tm, tn), lambda i, j: (i, jnp.clip(j - lo, 0, hi - lo - 1)))

    return pl.pallas_call(
        functools.partial(_norm_proj_kernel, bounds=tuple(bounds)),
        grid=(m // tm, lo),
        in_specs=[
            pl.BlockSpec((tm, d), lambda i, j: (i, 0)),
            pl.BlockSpec((1, d), lambda i, j: (0, 0)),
            pl.BlockSpec((d, tn), lambda i, j: (0, j)),
        ],
        out_specs=[out_spec(lo_, hi_) for lo_, hi_ in bounds],
        out_shape=[jax.ShapeDtypeStruct((m, nc), dt) for nc, dt in splits],
        scratch_shapes=[pltpu.VMEM((tm, d), BF16)],
        compiler_params=_params("arbitrary", "arbitrary"),
        name="norm_proj",
    )(x2, g, w)


def _t5_large_thresholds():
    half = N_BUCKETS // 2
    max_exact = half // 2
    n = np.arange(max_exact, 4 * MAX_DISTANCE, dtype=np.float64)
    large = max_exact + (np.log(n / max_exact) / math.log(MAX_DISTANCE / max_exact)
                         * (half - max_exact)).astype(np.int64)
    large = np.minimum(large, half - 1)
    thr = [int(n[np.argmax(large >= b)]) for b in range(max_exact + 1, half)]
    assert large[-1] == half - 1 and thr[-1] < MAX_DISTANCE
    return max_exact, half, thr


def _bias_tiles_kernel(rb_ref, o_ref, *, t):
    h = pl.program_id(0)
    max_exact, half, thr = _t5_large_thresholds()
    qi = lax.broadcasted_iota(jnp.int32, (t, t), 0)
    kj = lax.broadcasted_iota(jnp.int32, (t, t), 1)
    assert t + 1 >= thr[-1]
    for d in (0, 4):
        o_ref[d] = jnp.full((t, t), rb_ref[half - 1 + (half if d == 4 else 0), h], F32)
    for d in (1, 2, 3):
        rel = (d - 2) * t + kj - qi
        n = jnp.abs(rel)
        large = jnp.full((t, t), max_exact, jnp.int32)
        for th in thr:
            large = large + (n >= th).astype(jnp.int32)
        bucket = jnp.where(rel > 0, half, 0) + jnp.where(n < max_exact, n, large)
        tile = jnp.zeros((t, t), F32)
        for b in range(N_BUCKETS):
            tile = jnp.where(bucket == b, rb_ref[b, h], tile)
        o_ref[d] = tile


def _bias_tiles(rel_bias, t):
    assert t + 1 >= MAX_DISTANCE
    return pl.pallas_call(
        functools.partial(_bias_tiles_kernel, t=t),
        grid=(ATTN_HEADS,),
        in_specs=[pl.BlockSpec(memory_space=pltpu.SMEM)],
        out_specs=pl.BlockSpec((None, 5, t, t), lambda h: (h, 0, 0, 0)),
        out_shape=jax.ShapeDtypeStruct((ATTN_HEADS, 5, t, t), F32),
        compiler_params=_params("arbitrary"),
        name="t5_bias_tiles",
    )(rel_bias)


def _attn_kernel(q_ref, k_ref, v_ref, bias_ref, lam_ref, g_ref, *refs, t, nkb, nq, n_groups, per, n_cast):
    cast_in, o_ref, cast_out = refs[:n_cast], refs[n_cast], refs[n_cast + 1:2 * n_cast + 1]
    sa_ref, sb_ref, ma_ref, mb_ref, vext_ref = refs[2 * n_cast + 1:]
    step = pl.program_id(0)
    groups_per_head = nq // per
    qg = jnp.minimum(step, n_groups - 1) % groups_per_head
    done_group = jnp.maximum(step - 1, 0)

    @pl.when(step == 0)
    def _():
        sb_ref[...] = jnp.zeros_like(sb_ref)
        mb_ref[...] = jnp.zeros_like(mb_ref)

    @pl.when(done_group % groups_per_head == 0)
    def _():
        vext_ref[:, :HEAD_COLS] = v_ref[...]
        vext_ref[:, HEAD_COLS:] = jnp.ones((vext_ref.shape[0], HEAD_COLS), BF16)

    def body(cur_s, cur_m, prev_s, prev_m):
        for w_ref, wb_ref in zip(cast_in, cast_out):
            wb_ref[...] = w_ref[...].astype(BF16)
        lq1, lk1, lq2, lk2 = (lam_ref[i:i + 1, :] for i in range(4))
        lam = (jnp.exp(jnp.sum(lq1 * lk1, keepdims=True)) - jnp.exp(jnp.sum(lq2 * lk2, keepdims=True))
               + LAMBDA_INIT)

        def score(ii):
            q = q_ref[ii * t:(ii + 1) * t, :] * (ATTN_HEAD_DIM ** -0.5)
            lane = lax.broadcasted_iota(jnp.int32, q.shape, 1)
            qs = (jnp.where(lane < ATTN_HEAD_DIM, q, 0).astype(BF16),
                  jnp.where(lane >= ATTN_HEAD_DIM, q, 0).astype(BF16))
            qb = qg * per + ii
            m = [jnp.full((t, 1), -jnp.inf, F32) for _ in range(2)]
            for kb in range(nkb):
                kblk = k_ref[kb * t:(kb + 1) * t, :]
                bias = bias_ref[jnp.clip(kb - qb, -2, 2) + 2]
                for j in range(2):
                    s = _dot_nt(qs[j], kblk) + bias
                    cur_s[ii, j, :, kb * t:(kb + 1) * t] = s
                    m[j] = jnp.maximum(m[j], jnp.max(s, axis=-1, keepdims=True))
            for j in range(2):
                cur_m[ii, j] = m[j]

        def finish(ii):
            outs = []
            for j in range(2):
                mj = prev_m[ii, j]
                acc = jnp.zeros((t, 2 * HEAD_COLS), F32)
                for kb in range(nkb):
                    p = jnp.exp(prev_s[ii, j, :, kb * t:(kb + 1) * t] - mj).astype(BF16)
                    acc = acc + _dot(p, vext_ref[kb * t:(kb + 1) * t, :])
                outs.append(acc[:, :HEAD_COLS] / acc[:, HEAD_COLS:])
            att = outs[0] - lam * outs[1]
            ms = jnp.mean(att * att, axis=-1, keepdims=True)
            att = att * lax.rsqrt(ms + SUBLN_EPS) * g_ref[...] * (1.0 - LAMBDA_INIT)
            o_ref[ii * t:(ii + 1) * t, :] = att.astype(o_ref.dtype)

        for ii in range(per):
            score(ii)
            finish(ii)

    @pl.when(step % 2 == 0)
    def _():
        body(sa_ref, ma_ref, sb_ref, mb_ref)

    @pl.when(step % 2 == 1)
    def _():
        body(sb_ref, mb_ref, sa_ref, ma_ref)


def _cast_chunks(w, n_steps):
    rows, cols = w.shape
    per = V7X_BF16_SUBLANES
    while rows % per or rows // per > n_steps:
        per += V7X_BF16_SUBLANES
        assert per <= rows
    ncb = 1
    while (rows // per) * ncb * 2 <= n_steps and cols % (ncb * 2 * V7X_LANES) == 0:
        ncb *= 2
    return w.reshape(rows // per, per, cols), ncb


def _attention(qkv3, bias_tiles, lam_vecs, g_subln, t, cast_weights):
    b, l, _ = qkv3.shape
    nkb = nq = l // t
    per = 2 if nq % 2 == 0 else 1
    gph = nq // per
    n_groups = b * ATTN_HEADS * gph
    kcol0 = ATTN_HEADS
    vcol0 = 2 * ATTN_HEADS
    chunked = [_cast_chunks(w, n_groups + 1) for w in cast_weights]

    def cast_spec(w3, ncb):
        n = w3.shape[0] * ncb

        def index(s):
            c = jnp.minimum(s, n - 1)
            return c // ncb, 0, c % ncb

        return pl.BlockSpec((None, w3.shape[1], w3.shape[2] // ncb), index)

    def group(s):
        return s // (ATTN_HEADS * gph), (s // gph) % ATTN_HEADS, s % gph

    def scored(s):
        return group(jnp.minimum(s, n_groups - 1))

    def finished(s):
        return group(jnp.maximum(s - 1, 0))

    outs = pl.pallas_call(
        functools.partial(_attn_kernel, t=t, nkb=nkb, nq=nq, n_groups=n_groups, per=per,
                          n_cast=len(chunked)),
        grid=(n_groups + 1,),
        in_specs=[
            pl.BlockSpec((None, per * t, HEAD_COLS), lambda s: (scored(s)[0], scored(s)[2], scored(s)[1])),
            pl.BlockSpec((None, l, HEAD_COLS), lambda s: (scored(s)[0], 0, kcol0 + scored(s)[1])),
            pl.BlockSpec((None, l, HEAD_COLS), lambda s: (finished(s)[0], 0, vcol0 + finished(s)[1])),
            pl.BlockSpec((None, 5, t, t), lambda s: (scored(s)[1], 0, 0, 0)),
            pl.BlockSpec((4, ATTN_HEAD_DIM), lambda s: (0, 0)),
            pl.BlockSpec((1, HEAD_COLS), lambda s: (0, 0)),
        ] + [cast_spec(w3, ncb) for w3, ncb in chunked],
        out_specs=[pl.BlockSpec((None, per * t, HEAD_COLS),
                                lambda s: (finished(s)[0], finished(s)[2], finished(s)[1]))]
        + [cast_spec(w3, ncb) for w3, ncb in chunked],
        out_shape=[jax.ShapeDtypeStruct((b, l, ATTN_HEADS * HEAD_COLS), BF16)]
        + [jax.ShapeDtypeStruct(w3.shape, BF16) for w3, _ in chunked],
        scratch_shapes=[pltpu.VMEM((per, 2, t, l), F32), pltpu.VMEM((per, 2, t, l), F32),
                        pltpu.VMEM((per, 2, t, 1), F32), pltpu.VMEM((per, 2, t, 1), F32),
                        pltpu.VMEM((l, 2 * HEAD_COLS), BF16)],
        compiler_params=_params("arbitrary"),
        name="diff_attention",
    )(qkv3, qkv3, qkv3, bias_tiles, lam_vecs, g_subln, *[w3 for w3, _ in chunked])
    return outs[0], [wb.reshape(w.shape) for wb, w in zip(outs[1:], cast_weights)]


def _filter_kernel(w1t_ref, w1c_ref, w1s_ref, b1_ref, w2_ref, b2_ref, w3_ref, b3_ref, w4_ref, fr_ref,
                   cast_ref, o_ref, cast_out_ref, *, l, tl, width):
    cast_out_ref[...] = cast_ref[...].astype(BF16)
    i = pl.program_id(0)
    pos_row = (lax.broadcasted_iota(jnp.int32, (1, tl), 1) + i * tl).astype(F32)
    ang = 2.0 * math.pi * pos_row / l
    bi = lax.broadcasted_iota(jnp.int32, (FILTER_BANDS, 1), 0).astype(F32)
    bands = 1e-4 + bi * ((FILTER_BANDS - 1 - 1e-4) / (FILTER_BANDS - 1))
    pre = (w1t_ref[...] * (pos_row / (l - 1)) + _dot(w1c_ref[...], jnp.cos(bands * ang))
           - _dot(w1s_ref[...], jnp.sin(bands * ang)))
    fr = fr_ref[...]
    h = jnp.sin(fr * (pre + b1_ref[...]))
    h = jnp.sin(fr * (_dot(w2_ref[...], h) + b2_ref[...]))
    h = jnp.sin(fr * (_dot(w3_ref[...], h) + b3_ref[...]))
    h = lax.dot_general(h, w4_ref[...], (((0,), (0,)), ((), ())), preferred_element_type=F32)
    pos = (lax.broadcasted_iota(jnp.int32, (tl, 1), 0) + i * tl).astype(F32)
    t = pos / (l - 1)
    max_decay = math.log(DECAY_TARGET) / FAST_DECAY_PCT
    min_decay = math.log(DECAY_TARGET) / SLOW_DECAY_PCT
    ci = lax.broadcasted_iota(jnp.int32, (1, 2 * width), 1)
    ci = jnp.where(ci >= width, ci - width, ci).astype(F32)
    deltas = jnp.abs(min_decay + ci * ((max_decay - min_decay) / (width - 1)))
    o_ref[...] = h * jnp.exp(-t * deltas)


def _hyena_filter(l, w1, b1, w2, b2, w3, b3, w4, freq, tl, cast_weight):
    width = w4.shape[1] // 2
    steps = l // tl
    rows, cols = cast_weight.shape
    assert rows % (steps * V7X_BF16_SUBLANES) == 0
    cast_spec = pl.BlockSpec((rows // steps, cols), lambda i: (i, 0))
    full = lambda a: pl.BlockSpec(a.shape, lambda i: (0,) * a.ndim)
    args = (w1[0:1].T, w1[1:1 + FILTER_BANDS].T, w1[1 + FILTER_BANDS:].T, b1.T, w2.T, b2.T, w3.T, b3.T,
            w4, freq.T)
    return pl.pallas_call(
        functools.partial(_filter_kernel, l=l, tl=tl, width=width),
        grid=(steps,),
        in_specs=[full(a) for a in args] + [cast_spec],
        out_specs=[pl.BlockSpec((tl, 2 * width), lambda i: (i, 0)), cast_spec],
        out_shape=[jax.ShapeDtypeStruct((l, 2 * width), F32), jax.ShapeDtypeStruct((rows, cols), BF16)],
        compiler_params=_params("arbitrary"),
        name="hyena_filter",
    )(*args, cast_weight)


def _shifted_rows(z):
    n = z.shape[0]
    g = 8
    row = lax.broadcasted_iota(jnp.int32, (g, z.shape[1]), 0)
    prev = pltpu.roll(z, 1, 0)
    nxt = pltpu.roll(z, n - 1, 0)
    prev = jnp.concatenate([jnp.where(row == 0, 0.0, prev[:g]), prev[g:]], axis=0)
    nxt = jnp.concatenate([nxt[:n - g], jnp.where(row == g - 1, 0.0, nxt[n - g:])], axis=0)
    return prev, nxt


def _dwconv_rows(z, w, b):
    prev, nxt = _shifted_rows(z)
    return prev * w[0:1, :] + z * w[1:2, :] + nxt * w[2:3, :] + b


def _hy_gate_kernel(a0, a1, a2, w0, w1, w2, b0, b1, b2, x0_ref, u_ref):
    x0_ref[...] = _dwconv_rows(a0[...].astype(F32), w0[...], b0[...])
    u = (_dwconv_rows(a1[...].astype(F32), w1[...], b1[...])
         * _dwconv_rows(a2[...].astype(F32), w2[...], b2[...]))
    u_ref[...] = u.astype(u_ref.dtype)


def _hy_gate(hy3, conv_w, conv_b, ct):
    b, l, cols = hy3.shape
    c = cols // 3
    nct = c // ct
    a_spec = lambda k: pl.BlockSpec((None, l, ct), lambda bi, ci: (bi, 0, k * nct + ci))
    w_spec = lambda k: pl.BlockSpec((3, ct), lambda bi, ci: (0, k * nct + ci))
    b_spec = lambda k: pl.BlockSpec((1, ct), lambda bi, ci: (0, k * nct + ci))
    o_spec = pl.BlockSpec((None, l, ct), lambda bi, ci: (bi, 0, ci))
    return pl.pallas_call(
        _hy_gate_kernel,
        grid=(b, nct),
        in_specs=[a_spec(0), a_spec(1), a_spec(2), w_spec(0), w_spec(1), w_spec(2),
                  b_spec(0), b_spec(1), b_spec(2)],
        out_specs=[o_spec, o_spec],
        out_shape=[jax.ShapeDtypeStruct((b, l, c), F32), jax.ShapeDtypeStruct((b, l, c), BF16)],
        compiler_params=_params("arbitrary", "arbitrary"),
        name="hyena_gate",
    )(hy3, hy3, hy3, conv_w, conv_w, conv_w, conv_b, conv_b, conv_b)


FFT_N2 = V7X_MXU_DIM
FFT_R = V7X_BF16_SUBLANES


@functools.lru_cache(maxsize=None)
def _fft_constants(l):
    n = 2 * l
    n2 = FFT_N2
    n1 = n // n2
    assert n1 * n2 == n and n1 % 2 == 0
    r = FFT_R
    eye = np.eye(r)
    k1 = np.arange(n1)[:, None]
    a1 = 2.0 * np.pi * k1 * np.arange(n1 // 2)[None, :] / n1
    m1r = np.kron(np.cos(a1), eye)
    m1i = np.kron(-np.sin(a1), eye)
    m1ir = np.kron(np.cos(a1).T, eye) / n
    m1ii = np.kron(np.sin(a1).T, eye) / n
    a2 = 2.0 * np.pi * np.arange(n2)[:, None] * np.arange(n2)[None, :] / n2
    f2r, f2i = np.cos(a2), -np.sin(a2)
    at = 2.0 * np.pi * k1 * np.arange(n2)[None, :] / n
    twr = np.broadcast_to(np.cos(at)[:, :, None], (n1, n2, V7X_LANES))
    twi = np.broadcast_to(-np.sin(at)[:, :, None], (n1, n2, V7X_LANES))
    f = lambda a: np.ascontiguousarray(a, dtype=np.float32)
    mats = dict(
        m1_real=np.vstack([m1r, m1i]),
        m1_re=np.hstack([m1r, -m1i]), m1_im=np.hstack([m1i, m1r]),
        m1inv_re=np.hstack([m1ir, -m1ii]), m1inv_im=np.hstack([m1ii, m1ir]),
        f2_re=np.hstack([f2r, -f2i]), f2_im=np.hstack([f2i, f2r]),
        f2conj_re=np.hstack([f2r, f2i]), f2conj_im=np.hstack([-f2i, f2r]),
    )
    return dict(n1=n1, n2=n2, mats={k: f(v) for k, v in mats.items()}, tw=(f(twr), f(twi)))


FFT_UNROLL = 4


def _fft_tools(n1, ct, twr_ref, twi_ref, a_ref):
    r = FFT_R
    reps = ct // V7X_LANES
    rows_out = n1 * r

    def rows(ref_slice_fn, count):
        return jnp.concatenate([ref_slice_fn(i) for i in range(count)], axis=0)

    def twiddle(tr, ti):
        if reps == 1:
            return tr, ti
        return jnp.concatenate([tr] * reps, axis=-1), jnp.concatenate([ti] * reps, axis=-1)

    def blk(j):
        return pl.ds(pl.multiple_of(j * r, r), r)

    def strided_fwd(mats, g, j):
        if len(mats) == 1:
            ar, ai = _dot(mats[0][:rows_out], g), _dot(mats[0][rows_out:], g)
        else:
            ar, ai = _dot(mats[0][...], g), _dot(mats[1][...], g)
        tr, ti = twiddle(rows(lambda i: twr_ref[i, blk(j), :], n1), rows(lambda i: twi_ref[i, blk(j), :], n1))
        br = (ar * tr - ai * ti).astype(BF16)
        bi = (ar * ti + ai * tr).astype(BF16)
        for i in range(n1):
            a_ref[0, i, blk(j), :] = br[i * r:(i + 1) * r]
            a_ref[1, i, blk(j), :] = bi[i * r:(i + 1) * r]

    return rows, twiddle, blk, strided_fwd


def _hyena_spectrum_kernel(hf_ref, hb_ref, m1_real_ref, f2_re_ref, f2_im_ref, twr_ref, twi_ref,
                           k_ref, a_ref, *, n1, n2, ct):
    h1 = n1 // 2
    rows, _, blk, strided_fwd = _fft_tools(n1, ct, twr_ref, twi_ref, a_ref)
    for part in range(2):
        def stage1(j, _):
            hf = rows(lambda i: hf_ref[i, blk(j), :], h1)
            hb = rows(lambda i: hb_ref[i, blk(j), :], h1)
            g = (hf + hb) if part == 0 else (hf - hb)
            strided_fwd((m1_real_ref,), g.astype(BF16), j)
            return 0
        lax.fori_loop(0, n2 // FFT_R, stage1, 0, unroll=FFT_UNROLL)

        def stage2(i, _):
            a = jnp.concatenate([a_ref[0, i], a_ref[1, i]], axis=0)
            f2 = f2_re_ref if part == 0 else f2_im_ref
            k_ref[part, i] = _dot(f2[...], a).astype(k_ref.dtype)
            return 0
        lax.fori_loop(0, n1, stage2, 0, unroll=FFT_UNROLL)


def _hyena_conv_kernel(u_ref, k_ref, d_ref, m1_re_ref, m1_im_ref, m1inv_re_ref, m1inv_im_ref,
                       f2_re_ref, f2_im_ref, f2conj_re_ref, f2conj_im_ref, twr_ref, twi_ref,
                       o_ref, a_ref, *, n1, n2, ct):
    r = FFT_R
    h1 = n1 // 2
    nblk = n2 // r
    rows, twiddle, blk, strided_fwd = _fft_tools(n1, ct, twr_ref, twi_ref, a_ref)

    def u_stage1(j, _):
        g = rows(lambda i: u_ref[i // h1, i % h1, blk(j), :], 2 * h1)
        strided_fwd((m1_re_ref, m1_im_ref), g, j)
        return 0
    lax.fori_loop(0, nblk, u_stage1, 0, unroll=FFT_UNROLL)

    def forward(i):
        a = jnp.concatenate([a_ref[0, i], a_ref[1, i]], axis=0)
        return _dot(f2_re_ref[...], a), _dot(f2_im_ref[...], a)

    def inverse(i, xr, xi):
        kr, ki = k_ref[0, i].astype(F32), k_ref[1, i].astype(F32)
        y = jnp.concatenate([(xr * kr - xi * ki).astype(BF16), (xr * ki + xi * kr).astype(BF16)], axis=0)
        br, bi = _dot(f2conj_re_ref[...], y), _dot(f2conj_im_ref[...], y)
        tr, ti = twiddle(twr_ref[i], twi_ref[i])
        a_ref[0, i] = (br * tr + bi * ti).astype(BF16)
        a_ref[1, i] = (bi * tr - br * ti).astype(BF16)

    def mid(g, _):
        base = g * FFT_UNROLL
        pending = forward(base)
        for j in range(FFT_UNROLL):
            nxt = forward(base + j + 1) if j + 1 < FFT_UNROLL else None
            inverse(base + j, *pending)
            pending = nxt
        return 0
    assert n1 % FFT_UNROLL == 0
    lax.fori_loop(0, n1 // FFT_UNROLL, mid, 0)

    def last(j, _):
        bb = rows(lambda i: a_ref[i // n1, i % n1, blk(j), :], 2 * n1)
        y = (_dot(m1inv_re_ref[...], bb), _dot(m1inv_im_ref[...], bb))
        d = d_ref[...]
        for bidx in range(2):
            for i in range(h1):
                u = u_ref[bidx, i, blk(j), :].astype(F32)
                o_ref[bidx, i, blk(j), :] = (y[bidx][i * r:(i + 1) * r] + u * d).astype(o_ref.dtype)
        return 0
    lax.fori_loop(0, nblk, last, 0, unroll=FFT_UNROLL)


def _hyena_conv(u, filt, hy_d, ct):
    b, l, c = u.shape
    assert b == 2, "the two batch rows are packed as real / imaginary parts of one complex transform"
    consts = _fft_constants(l)
    n1, n2 = consts["n1"], consts["n2"]
    h1 = n1 // 2
    nct = c // ct
    u4 = u.reshape(b, h1, n2, c)
    f3 = filt.reshape(h1, n2, 2 * c)
    mats = {k: jnp.asarray(v, dtype=BF16) for k, v in consts["mats"].items()}
    spec_consts = [mats[k] for k in ("m1_real", "f2_re", "f2_im")]
    conv_consts = [mats[k] for k in ("m1_re", "m1_im", "m1inv_re", "m1inv_im",
                                     "f2_re", "f2_im", "f2conj_re", "f2conj_im")]
    tws = [jnp.asarray(a) for a in consts["tw"]]
    const_spec = lambda a: pl.BlockSpec(a.shape, lambda ci: (0,) * a.ndim, pipeline_mode=pl.Buffered(1))
    spectrum = pl.pallas_call(
        functools.partial(_hyena_spectrum_kernel, n1=n1, n2=n2, ct=ct),
        grid=(nct,),
        in_specs=[
            pl.BlockSpec((h1, n2, ct), lambda ci: (0, 0, ci)),
            pl.BlockSpec((h1, n2, ct), lambda ci: (0, 0, nct + ci)),
        ] + [const_spec(a) for a in (*spec_consts, *tws)],
        out_specs=pl.BlockSpec((2, n1, n2, ct), lambda ci: (0, 0, 0, ci)),
        out_shape=jax.ShapeDtypeStruct((2, n1, n2, c), BF16),
        scratch_shapes=[pltpu.VMEM((2, n1, n2, ct), BF16)],
        compiler_params=_params("arbitrary"),
        name="hyena_filter_spectrum",
    )(f3, f3, *spec_consts, *tws)
    out = pl.pallas_call(
        functools.partial(_hyena_conv_kernel, n1=n1, n2=n2, ct=ct),
        grid=(nct,),
        in_specs=[
            pl.BlockSpec((b, h1, n2, ct), lambda ci: (0, 0, 0, ci)),
            pl.BlockSpec((2, n1, n2, ct), lambda ci: (0, 0, 0, ci)),
            pl.BlockSpec((1, ct), lambda ci: (0, ci)),
        ] + [const_spec(a) for a in (*conv_consts, *tws)],
        out_specs=pl.BlockSpec((b, h1, n2, ct), lambda ci: (0, 0, 0, ci)),
        out_shape=jax.ShapeDtypeStruct((b, h1, n2, c), BF16),
        scratch_shapes=[pltpu.VMEM((2, n1, n2, ct), BF16)],
        compiler_params=_params("arbitrary"),
        name="hyena_long_conv",
    )(u4, spectrum, hy_d, *conv_consts, *tws)
    return out.reshape(b * l, c)


def _merge_kernel(att_ref, yc_ref, x0_ref, ga_ref, gh_ref, x_ref, wa_ref, wh_ref, wo_ref, g_ref, x1_ref, hf_ref):
    a = _dot(att_ref[...], wa_ref[...])
    yhy = x0_ref[...] * yc_ref[...].astype(F32)
    h = _dot(yhy.astype(BF16), wh_ref[...])
    merged = jax.nn.sigmoid(ga_ref[...].astype(F32)) * a + jax.nn.sigmoid(gh_ref[...].astype(F32)) * h
    x1 = x_ref[...] + _dot(merged.astype(BF16), wo_ref[...])
    x1_ref[...] = x1
    ms = jnp.mean(x1 * x1, axis=-1, keepdims=True)
    hf_ref[...] = (x1 * lax.rsqrt(ms + NORM_EPS) * g_ref[...]).astype(BF16)


def _merge(att, yc, x0, gates, x2, wa, wh, wo, g_ffn, tm):
    m, d = x2.shape
    row = lambda w: pl.BlockSpec((tm, w), lambda i: (i, 0))
    const = lambda a: pl.BlockSpec(a.shape, lambda i: (0, 0), pipeline_mode=pl.Buffered(1))
    return pl.pallas_call(
        _merge_kernel,
        grid=(m // tm,),
        in_specs=[row(att.shape[1]), row(yc.shape[1]), row(x0.shape[1]),
                  pl.BlockSpec((tm, d), lambda i: (i, 0)), pl.BlockSpec((tm, d), lambda i: (i, 1)),
                  row(d), const(wa), const(wh), const(wo), const(g_ffn)],
        out_specs=[row(d), row(d)],
        out_shape=[jax.ShapeDtypeStruct((m, d), F32), jax.ShapeDtypeStruct((m, d), BF16)],
        compiler_params=_params("arbitrary"),
        name="merge_out_proj",
    )(att, yc, x0, gates, gates, x2, wa, wh, wo, g_ffn)


FFN_HALO = V7X_BF16_SUBLANES


def _ffn_kernel(hf_ref, prev_ref, next_ref, x1_ref, wg_ref, wv_ref, cwg_ref, cwv_ref, cbg_ref, cbv_ref,
                wd_ref, g_ref, o_ref, ext_ref, acc_ref, *, tm, tiles_per_seq):
    i = pl.program_id(0)
    f = pl.program_id(1)
    hl = FFN_HALO

    @pl.when(f == 0)
    def _():
        first = (i % tiles_per_seq) == 0
        last = (i % tiles_per_seq) == tiles_per_seq - 1
        ext_ref[0:hl, :] = jnp.where(first, jnp.zeros_like(prev_ref[...]), prev_ref[...])
        ext_ref[hl:hl + tm, :] = hf_ref[...]
        ext_ref[hl + tm:, :] = jnp.where(last, jnp.zeros_like(next_ref[...]), next_ref[...])
        acc_ref[...] = jnp.zeros_like(acc_ref)

    ext = ext_ref[...]
    ne = tm + 2 * hl

    def conv(w_ref, cw_ref, cb_ref):
        up = _dot(ext, w_ref[...])
        cw = cw_ref[...]
        prev = pltpu.roll(up, 1, 0)
        nxt = pltpu.roll(up, ne - 1, 0)
        full = prev * cw[0:1, :] + up * cw[1:2, :] + nxt * cw[2:3, :] + cb_ref[...]
        return full[hl:hl + tm]

    gate = conv(wg_ref, cwg_ref, cbg_ref)
    val = conv(wv_ref, cwv_ref, cbv_ref)
    act = (gate * jax.nn.sigmoid(gate) * val).astype(BF16)
    acc_ref[...] += _dot(act, wd_ref[...])

    @pl.when(f == pl.num_programs(1) - 1)
    def _():
        x2 = x1_ref[...] + acc_ref[...]
        ms = jnp.mean(x2 * x2, axis=-1, keepdims=True)
        o_ref[...] = x2 * lax.rsqrt(ms + NORM_EPS) * g_ref[...]


def _ffn(hf, x1, w_up, conv_w, conv_b, w_down, g_final, l, tm, tf):
    m, d = hf.shape
    dff = w_down.shape[0]
    nf = dff // tf
    hl = FFN_HALO
    assert l % tm == 0 and tm % hl == 0 and dff % tf == 0
    hb = tm // hl
    nhb = m // hl
    return pl.pallas_call(
        functools.partial(_ffn_kernel, tm=tm, tiles_per_seq=l // tm),
        grid=(m // tm, nf),
        in_specs=[
            pl.BlockSpec((tm, d), lambda i, f: (i, 0)),
            pl.BlockSpec((hl, d), lambda i, f: (jnp.maximum(i * hb - 1, 0), 0)),
            pl.BlockSpec((hl, d), lambda i, f: (jnp.minimum((i + 1) * hb, nhb - 1), 0)),
            pl.BlockSpec((tm, d), lambda i, f: (i, 0)),
            pl.BlockSpec((d, tf), lambda i, f: (0, f)),
            pl.BlockSpec((d, tf), lambda i, f: (0, nf + f)),
            pl.BlockSpec((3, tf), lambda i, f: (0, f)),
            pl.BlockSpec((3, tf), lambda i, f: (0, nf + f)),
            pl.BlockSpec((1, tf), lambda i, f: (0, f)),
            pl.BlockSpec((1, tf), lambda i, f: (0, nf + f)),
            pl.BlockSpec((tf, d), lambda i, f: (f, 0)),
            pl.BlockSpec((1, d), lambda i, f: (0, 0)),
        ],
        out_specs=pl.BlockSpec((tm, d), lambda i, f: (i, 0)),
        out_shape=jax.ShapeDtypeStruct((m, d), F32),
        scratch_shapes=[pltpu.VMEM((tm + 2 * hl, d), BF16), pltpu.VMEM((tm, d), F32)],
        compiler_params=_params("arbitrary", "arbitrary"),
        name="conv_ffn",
    )(hf, hf, hf, x1, w_up, w_up, conv_w, conv_w, conv_b, conv_b, w_down, g_final)


def _tile(n, want):
    t = min(n, want)
    assert n % t == 0
    return t


def kernel(x, g_mix, w_in, lambda_q1, lambda_k1, lambda_q2, lambda_k2, g_subln, rel_bias, hy_conv_w,
           hy_conv_b, hy_f_w1, hy_f_b1, hy_f_w2, hy_f_b2, hy_f_w3, hy_f_b3, hy_f_w4, hy_freq, hy_d,
           w_attn_branch, w_hyena_branch, w_out, g_ffn, w_up, ffn_conv_w, ffn_conv_b, w_down, g_final):
    b, l, d = x.shape
    m = b * l
    depth = w_in.shape[0]
    assert depth == 1
    qkv_cols = 3 * ATTN_HEADS * HEAD_COLS
    width = hy_d.shape[1]
    hy_cols = 3 * width
    gate_cols = 2 * d
    assert w_in.shape[2] == qkv_cols + hy_cols + gate_cols

    x2 = x.reshape(m, d)
    row = lambda v: v.reshape(1, -1)

    filt, w_in_b = _hyena_filter(l, hy_f_w1[0], hy_f_b1, hy_f_w2[0], hy_f_b2, hy_f_w3[0], hy_f_b3, hy_f_w4[0],
                                 hy_freq, _tile(l, 512), w_in[0])

    qkv, hyp, gates = _norm_proj(x2, g_mix, w_in_b, ((qkv_cols, BF16), (hy_cols, BF16), (gate_cols, BF16)),
                                 _tile(m, 1024), math.gcd(qkv_cols, hy_cols, gate_cols, 1024))

    t_attn = _tile(l, 256)
    bias_tiles = _bias_tiles(rel_bias, t_attn)
    lam_vecs = jnp.concatenate([lambda_q1, lambda_k1, lambda_q2, lambda_k2], axis=0)
    att, (wa_b, wh_b, wo_b, w_up_b, w_down_b) = _attention(
        qkv.reshape(b, l, qkv_cols), bias_tiles, lam_vecs, g_subln, t_attn,
        (w_attn_branch[0], w_hyena_branch[0], w_out[0], w_up[0], w_down[0]))
    att = att.reshape(m, ATTN_HEADS * HEAD_COLS)

    x0, u = _hy_gate(hyp.reshape(b, l, hy_cols), hy_conv_w[0], hy_conv_b, _tile(width, V7X_LANES))
    yc = _hyena_conv(u, filt, hy_d, _tile(width, V7X_MXU_DIM))

    x1, hf = _merge(att, yc, x0.reshape(m, width), gates, x2, wa_b, wh_b, wo_b, g_ffn, _tile(m, 512))

    out = _ffn(hf, x1, w_up_b, ffn_conv_w[0], ffn_conv_b, w_down_b,
               row(g_final), l, _tile(l, 512), _tile(w_down.shape[1], 512))
    return out.reshape(b, l, d)
```

```python
import functools
import math

import numpy as np
import jax
import jax.numpy as jnp
from jax import lax
from jax.experimental import pallas as pl
from jax.experimental.pallas import tpu as pltpu

ATTN_HEADS = 8
ATTN_HEAD_DIM = 64
HEAD_COLS = 2 * ATTN_HEAD_DIM
N_BUCKETS = 32
MAX_DISTANCE = 128
NORM_EPS = 1e-6
SUBLN_EPS = 1e-5
LAMBDA_INIT = 0.8 - 0.6 * math.exp(-0.3 * 0)
FILTER_EMB_DIM = 33
FILTER_BANDS = (FILTER_EMB_DIM - 1) // 2
FAST_DECAY_PCT = 0.3
SLOW_DECAY_PCT = 1.5
DECAY_TARGET = 1e-2

V7X_LANES = 128
V7X_BF16_SUBLANES = 16
V7X_MXU_DIM = 256
V7X_VMEM_LIMIT_BYTES = 60 * 1024 * 1024

BF16 = jnp.bfloat16
F32 = jnp.float32


def _params(*sem):
    return pltpu.CompilerParams(dimension_semantics=sem, vmem_limit_bytes=V7X_VMEM_LIMIT_BYTES)


def _dot(a, b):
    return jnp.dot(a, b, preferred_element_type=F32)


def _dot_nt(a, b):
    return lax.dot_general(a, b, (((1,), (1,)), ((), ())), preferred_element_type=F32)


def _norm_proj_kernel(x_ref, g_ref, w_ref, *refs, bounds):
    out_refs, xn_ref = refs[:-1], refs[-1]
    j = pl.program_id(1)

    @pl.when(j == 0)
    def _():
        x = x_ref[...]
        ms = jnp.mean(x * x, axis=-1, keepdims=True)
        xn_ref[...] = (x * lax.rsqrt(ms + NORM_EPS) * g_ref[...]).astype(BF16)

    for o_ref, (lo, hi) in zip(out_refs, bounds):
        @pl.when((j >= lo) & (j < hi))
        def _(o_ref=o_ref):
            o_ref[...] = _dot(xn_ref[...], w_ref[...]).astype(o_ref.dtype)


def _norm_proj(x2, g, w, splits, tm, tn):
    m, d = x2.shape
    assert m % tm == 0 and all(nc % tn == 0 for nc, _ in splits)
    bounds, lo = [], 0
    for nc, _ in splits:
        bounds.append((lo, lo + nc // tn))
        lo += nc // tn
    assert lo * tn == w.shape[1]

    def out_spec(lo, hi):
        return pl.BlockSpec((tm, tn), lambda i, j: (i, jnp.clip(j - lo, 0, hi - lo - 1)))

    return pl.pallas_call(
        functools.partial(_norm_proj_kernel, bounds=tuple(bounds)),
        grid=(m // tm, lo),
        in_specs=[
            pl.BlockSpec((tm, d), lambda i, j: (i, 0)),
            pl.BlockSpec((1, d), lambda i, j: (0, 0)),
            pl.BlockSpec((d, tn), lambda i, j: (0, j)),
        ],
        out_specs=[out_spec(lo_, hi_) for lo_, hi_ in bounds],
        out_shape=[jax.ShapeDtypeStruct((m, nc), dt) for nc, dt in splits],
        scratch_shapes=[pltpu.VMEM((tm, d), BF16)],
        compiler_params=_params("arbitrary", "arbitrary"),
        name="norm_proj",
    )(x2, g, w)


def _t5_large_thresholds():
    half = N_BUCKETS // 2
    max_exact = half // 2
    n = np.arange(max_exact, 4 * MAX_DISTANCE, dtype=np.float64)
    large = max_exact + (np.log(n / max_exact) / math.log(MAX_DISTANCE / max_exact)
                         * (half - max_exact)).astype(np.int64)
    large = np.minimum(large, half - 1)
    thr = [int(n[np.argmax(large >= b)]) for b in range(max_exact + 1, half)]
    assert large[-1] == half - 1 and thr[-1] < MAX_DISTANCE
    return max_exact, half, thr


def _bias_tiles_kernel(rb_ref, o_ref, *, t):
    h = pl.program_id(0)
    max_exact, half, thr = _t5_large_thresholds()
    qi = lax.broadcasted_iota(jnp.int32, (t, t), 0)
    kj = lax.broadcasted_iota(jnp.int32, (t, t), 1)
    assert t + 1 >= thr[-1]
    for d in (0, 4):
        o_ref[d] = jnp.full((t, t), rb_ref[half - 1 + (half if d == 4 else 0), h], F32)
    for d in (1, 2, 3):
        rel = (d - 2) * t + kj - qi
        n = jnp.abs(rel)
        large = jnp.full((t, t), max_exact, jnp.int32)
        for th in thr:
            large = large + (n >= th).astype(jnp.int32)
        bucket = jnp.where(rel > 0, half, 0) + jnp.where(n < max_exact, n, large)
        tile = jnp.zeros((t, t), F32)
        for b in range(N_BUCKETS):
            tile = jnp.where(bucket == b, rb_ref[b, h], tile)
        o_ref[d] = tile


def _bias_tiles(rel_bias, t):
    assert t + 1 >= MAX_DISTANCE
    return pl.pallas_call(
        functools.partial(_bias_tiles_kernel, t=t),
        grid=(ATTN_HEADS,),
        in_specs=[pl.BlockSpec(memory_space=pltpu.SMEM)],
        out_specs=pl.BlockSpec((None, 5, t, t), lambda h: (h, 0, 0, 0)),
        out_shape=jax.ShapeDtypeStruct((ATTN_HEADS, 5, t, t), F32),
        compiler_params=_params("arbitrary"),
        name="t5_bias_tiles",
    )(rel_bias)


def _attn_kernel(q_ref, k_ref, v_ref, bias_ref, lam_ref, g_ref, *refs, t, nkb, nq, n_groups, per, n_cast):
    cast_in, o_ref, cast_out = refs[:n_cast], refs[n_cast], refs[n_cast + 1:2 * n_cast + 1]
    sa_ref, sb_ref, ma_ref, mb_ref, vext_ref = refs[2 * n_cast + 1:]
    step = pl.program_id(0)
    groups_per_head = nq // per
    qg = jnp.minimum(step, n_groups - 1) % groups_per_head
    done_group = jnp.maximum(step - 1, 0)

    @pl.when(step == 0)
    def _():
        sb_ref[...] = jnp.zeros_like(sb_ref)
        mb_ref[...] = jnp.zeros_like(mb_ref)

    @pl.when(done_group % groups_per_head == 0)
    def _():
        vext_ref[:, :HEAD_COLS] = v_ref[...]
        vext_ref[:, HEAD_COLS:] = jnp.ones((vext_ref.shape[0], HEAD_COLS), BF16)

    def body(cur_s, cur_m, prev_s, prev_m):
        for w_ref, wb_ref in zip(cast_in, cast_out):
            wb_ref[...] = w_ref[...].astype(BF16)
        lq1, lk1, lq2, lk2 = (lam_ref[i:i + 1, :] for i in range(4))
        lam = (jnp.exp(jnp.sum(lq1 * lk1, keepdims=True)) - jnp.exp(jnp.sum(lq2 * lk2, keepdims=True))
               + LAMBDA_INIT)

        def score(ii):
            q = q_ref[ii * t:(ii + 1) * t, :] * (ATTN_HEAD_DIM ** -0.5)
            lane = lax.broadcasted_iota(jnp.int32, q.shape, 1)
            qs = (jnp.where(lane < ATTN_HEAD_DIM, q, 0).astype(BF16),
                  jnp.where(lane >= ATTN_HEAD_DIM, q, 0).astype(BF16))
            qb = qg * per + ii
            m = [jnp.full((t, 1), -jnp.inf, F32) for _ in range(2)]
            for kb in range(nkb):
                kblk = k_ref[kb * t:(kb + 1) * t, :]
                bias = bias_ref[jnp.clip(kb - qb, -2, 2) + 2]
                for j in range(2):
                    s = _dot_nt(qs[j], kblk) + bias
                    cur_s[ii, j, :, kb * t:(kb + 1) * t] = s
                    m[j] = jnp.maximum(m[j], jnp.max(s, axis=-1, keepdims=True))
            for j in range(2):
                cur_m[ii, j] = m[j]

        def finish(ii):
            outs = []
            for j in range(2):
                mj = prev_m[ii, j]
                acc = jnp.zeros((t, 2 * HEAD_COLS), F32)
                for kb in range(nkb):
                    p = jnp.exp(prev_s[ii, j, :, kb * t:(kb + 1) * t] - mj).astype(BF16)
                    acc = acc + _dot(p, vext_ref[kb * t:(kb + 1) * t, :])
                outs.append(acc[:, :HEAD_COLS] / acc[:, HEAD_COLS:])
            att = outs[0] - lam * outs[1]
            ms = jnp.mean(att * att, axis=-1, keepdims=True)
            att = att * lax.rsqrt(ms + SUBLN_EPS) * g_ref[...] * (1.0 - LAMBDA_INIT)
            o_ref[ii * t:(ii + 1) * t, :] = att.astype(o_ref.dtype)

        for ii in range(per):
            finish(ii)
            score(ii)

    @pl.when(step % 2 == 0)
    def _():
        body(sa_ref, ma_ref, sb_ref, mb_ref)

    @pl.when(step % 2 == 1)
    def _():
        body(sb_ref, mb_ref, sa_ref, ma_ref)


def _cast_chunks(w, n_steps):
    rows, cols = w.shape
    per = V7X_BF16_SUBLANES
    while rows % per or rows // per > n_steps:
        per += V7X_BF16_SUBLANES
        assert per <= rows
    ncb = 1
    while (rows // per) * ncb * 2 <= n_steps and cols % (ncb * 2 * V7X_LANES) == 0:
        ncb *= 2
    return w.reshape(rows // per, per, cols), ncb


def _attention(qkv3, bias_tiles, lam_vecs, g_subln, t, cast_weights):
    b, l, _ = qkv3.shape
    nkb = nq = l // t
    per = 2 if nq % 2 == 0 else 1
    gph = nq // per
    n_groups = b * ATTN_HEADS * gph
    kcol0 = ATTN_HEADS
    vcol0 = 2 * ATTN_HEADS
    chunked = [_cast_chunks(w, n_groups + 1) for w in cast_weights]

    def cast_spec(w3, ncb):
        n = w3.shape[0] * ncb

        def index(s):
            c = jnp.minimum(s, n - 1)
            return c // ncb, 0, c % ncb

        return pl.BlockSpec((None, w3.shape[1], w3.shape[2] // ncb), index)

    def group(s):
        return s // (ATTN_HEADS * gph), (s // gph) % ATTN_HEADS, s % gph

    def scored(s):
        return group(jnp.minimum(s, n_groups - 1))

    def finished(s):
        return group(jnp.maximum(s - 1, 0))

    outs = pl.pallas_call(
        functools.partial(_attn_kernel, t=t, nkb=nkb, nq=nq, n_groups=n_groups, per=per,
                          n_cast=len(chunked)),
        grid=(n_groups + 1,),
        in_specs=[
            pl.BlockSpec((None, per * t, HEAD_COLS), lambda s: (scored(s)[0], scored(s)[2], scored(s)[1])),
            pl.BlockSpec((None, l, HEAD_COLS), lambda s: (scored(s)[0], 0, kcol0 + scored(s)[1])),
            pl.BlockSpec((None, l, HEAD_COLS), lambda s: (finished(s)[0], 0, vcol0 + finished(s)[1])),
            pl.BlockSpec((None, 5, t, t), lambda s: (scored(s)[1], 0, 0, 0)),
            pl.BlockSpec((4, ATTN_HEAD_DIM), lambda s: (0, 0)),
            pl.BlockSpec((1, HEAD_COLS), lambda s: (0, 0)),
        ] + [cast_spec(w3, ncb) for w3, ncb in chunked],
        out_specs=[pl.BlockSpec((None, per * t, HEAD_COLS),
                                lambda s: (finished(s)[0], finished(s)[2], finished(s)[1]))]
        + [cast_spec(w3, ncb) for w3, ncb in chunked],
        out_shape=[jax.ShapeDtypeStruct((b, l, ATTN_HEADS * HEAD_COLS), BF16)]
        + [jax.ShapeDtypeStruct(w3.shape, BF16) for w3, _ in chunked],
        scratch_shapes=[pltpu.VMEM((per, 2, t, l), F32), pltpu.VMEM((per, 2, t, l), F32),
                        pltpu.VMEM((per, 2, t, 1), F32), pltpu.VMEM((per, 2, t, 1), F32),
                        pltpu.VMEM((l, 2 * HEAD_COLS), BF16)],
        compiler_params=_params("arbitrary"),
        name="diff_attention",
    )(qkv3, qkv3, qkv3, bias_tiles, lam_vecs, g_subln, *[w3 for w3, _ in chunked])
    return outs[0], [wb.reshape(w.shape) for wb, w in zip(outs[1:], cast_weights)]


def _filter_kernel(w1t_ref, w1c_ref, w1s_ref, b1_ref, w2_ref, b2_ref, w3_ref, b3_ref, w4_ref, fr_ref,
                   cast_ref, o_ref, cast_out_ref, *, l, tl, width):
    cast_out_ref[...] = cast_ref[...].astype(BF16)
    i = pl.program_id(0)
    pos_row = (lax.broadcasted_iota(jnp.int32, (1, tl), 1) + i * tl).astype(F32)
    ang = 2.0 * math.pi * pos_row / l
    bi = lax.broadcasted_iota(jnp.int32, (FILTER_BANDS, 1), 0).astype(F32)
    bands = 1e-4 + bi * ((FILTER_BANDS - 1 - 1e-4) / (FILTER_BANDS - 1))
    pre = (w1t_ref[...] * (pos_row / (l - 1)) + _dot(w1c_ref[...], jnp.cos(bands * ang))
           - _dot(w1s_ref[...], jnp.sin(bands * ang)))
    fr = fr_ref[...]
    h = jnp.sin(fr * (pre + b1_ref[...]))
    h = jnp.sin(fr * (_dot(w2_ref[...], h) + b2_ref[...]))
    h = jnp.sin(fr * (_dot(w3_ref[...], h) + b3_ref[...]))
    h = lax.dot_general(h, w4_ref[...], (((0,), (0,)), ((), ())), preferred_element_type=F32)
    pos = (lax.broadcasted_iota(jnp.int32, (tl, 1), 0) + i * tl).astype(F32)
    t = pos / (l - 1)
    max_decay = math.log(DECAY_TARGET) / FAST_DECAY_PCT
    min_decay = math.log(DECAY_TARGET) / SLOW_DECAY_PCT
    ci = lax.broadcasted_iota(jnp.int32, (1, 2 * width), 1)
    ci = jnp.where(ci >= width, ci - width, ci).astype(F32)
    deltas = jnp.abs(min_decay + ci * ((max_decay - min_decay) / (width - 1)))
    o_ref[...] = h * jnp.exp(-t * deltas)


def _hyena_filter(l, w1, b1, w2, b2, w3, b3, w4, freq, tl, cast_weight):
    width = w4.shape[1] // 2
    steps = l // tl
    rows, cols = cast_weight.shape
    assert rows % (steps * V7X_BF16_SUBLANES) == 0
    cast_spec = pl.BlockSpec((rows // steps, cols), lambda i: (i, 0))
    full = lambda a: pl.BlockSpec(a.shape, lambda i: (0,) * a.ndim)
    args = (w1[0:1].T, w1[1:1 + FILTER_BANDS].T, w1[1 + FILTER_BANDS:].T, b1.T, w2.T, b2.T, w3.T, b3.T,
            w4, freq.T)
    return pl.pallas_call(
        functools.partial(_filter_kernel, l=l, tl=tl, width=width),
        grid=(steps,),
        in_specs=[full(a) for a in args] + [cast_spec],
        out_specs=[pl.BlockSpec((tl, 2 * width), lambda i: (i, 0)), cast_spec],
        out_shape=[jax.ShapeDtypeStruct((l, 2 * width), F32), jax.ShapeDtypeStruct((rows, cols), BF16)],
        compiler_params=_params("arbitrary"),
        name="hyena_filter",
    )(*args, cast_weight)


def _shifted_rows(z):
    n = z.shape[0]
    g = 8
    row = lax.broadcasted_iota(jnp.int32, (g, z.shape[1]), 0)
    prev = pltpu.roll(z, 1, 0)
    nxt = pltpu.roll(z, n - 1, 0)
    prev = jnp.concatenate([jnp.where(row == 0, 0.0, prev[:g]), prev[g:]], axis=0)
    nxt = jnp.concatenate([nxt[:n - g], jnp.where(row == g - 1, 0.0, nxt[n - g:])], axis=0)
    return prev, nxt


def _dwconv_rows(z, w, b):
    prev, nxt = _shifted_rows(z)
    return prev * w[0:1, :] + z * w[1:2, :] + nxt * w[2:3, :] + b


def _hy_gate_kernel(a0, a1, a2, w0, w1, w2, b0, b1, b2, x0_ref, u_ref):
    x0_ref[...] = _dwconv_rows(a0[...].astype(F32), w0[...], b0[...])
    u = (_dwconv_rows(a1[...].astype(F32), w1[...], b1[...])
         * _dwconv_rows(a2[...].astype(F32), w2[...], b2[...]))
    u_ref[...] = u.astype(u_ref.dtype)


def _hy_gate(hy3, conv_w, conv_b, ct):
    b, l, cols = hy3.shape
    c = cols // 3
    nct = c // ct
    a_spec = lambda k: pl.BlockSpec((None, l, ct), lambda bi, ci: (bi, 0, k * nct + ci))
    w_spec = lambda k: pl.BlockSpec((3, ct), lambda bi, ci: (0, k * nct + ci))
    b_spec = lambda k: pl.BlockSpec((1, ct), lambda bi, ci: (0, k * nct + ci))
    o_spec = pl.BlockSpec((None, l, ct), lambda bi, ci: (bi, 0, ci))
    return pl.pallas_call(
        _hy_gate_kernel,
        grid=(b, nct),
        in_specs=[a_spec(0), a_spec(1), a_spec(2), w_spec(0), w_spec(1), w_spec(2),
                  b_spec(0), b_spec(1), b_spec(2)],
        out_specs=[o_spec, o_spec],
        out_shape=[jax.ShapeDtypeStruct((b, l, c), F32), jax.ShapeDtypeStruct((b, l, c), BF16)],
        compiler_params=_params("arbitrary", "arbitrary"),
        name="hyena_gate",
    )(hy3, hy3, hy3, conv_w, conv_w, conv_w, conv_b, conv_b, conv_b)


FFT_N2 = V7X_MXU_DIM
FFT_R = V7X_BF16_SUBLANES


@functools.lru_cache(maxsize=None)
def _fft_constants(l):
    n = 2 * l
    n2 = FFT_N2
    n1 = n // n2
    assert n1 * n2 == n and n1 % 2 == 0
    r = FFT_R
    eye = np.eye(r)
    k1 = np.arange(n1)[:, None]
    a1 = 2.0 * np.pi * k1 * np.arange(n1 // 2)[None, :] / n1
    m1r = np.kron(np.cos(a1), eye)
    m1i = np.kron(-np.sin(a1), eye)
    m1ir = np.kron(np.cos(a1).T, eye) / n
    m1ii = np.kron(np.sin(a1).T, eye) / n
    a2 = 2.0 * np.pi * np.arange(n2)[:, None] * np.arange(n2)[None, :] / n2
    f2r, f2i = np.cos(a2), -np.sin(a2)
    at = 2.0 * np.pi * k1 * np.arange(n2)[None, :] / n
    twr = np.broadcast_to(np.cos(at)[:, :, None], (n1, n2, V7X_LANES))
    twi = np.broadcast_to(-np.sin(at)[:, :, None], (n1, n2, V7X_LANES))
    f = lambda a: np.ascontiguousarray(a, dtype=np.float32)
    mats = dict(
        m1_real=np.vstack([m1r, m1i]),
        m1_re=np.hstack([m1r, -m1i]), m1_im=np.hstack([m1i, m1r]),
        m1inv_re=np.hstack([m1ir, -m1ii]), m1inv_im=np.hstack([m1ii, m1ir]),
        f2_re=np.hstack([f2r, -f2i]), f2_im=np.hstack([f2i, f2r]),
        f2conj_re=np.hstack([f2r, f2i]), f2conj_im=np.hstack([-f2i, f2r]),
    )
    return dict(n1=n1, n2=n2, mats={k: f(v) for k, v in mats.items()}, tw=(f(twr), f(twi)))


FFT_UNROLL = 4


def _fft_tools(n1, ct, twr_ref, twi_ref, a_ref):
    r = FFT_R
    reps = ct // V7X_LANES
    rows_out = n1 * r

    def rows(ref_slice_fn, count):
        return jnp.concatenate([ref_slice_fn(i) for i in range(count)], axis=0)

    def twiddle(tr, ti):
        if reps == 1:
            return tr, ti
        return jnp.concatenate([tr] * reps, axis=-1), jnp.concatenate([ti] * reps, axis=-1)

    def blk(j):
        return pl.ds(pl.multiple_of(j * r, r), r)

    def strided_fwd(mats, g, j):
        if len(mats) == 1:
            ar, ai = _dot(mats[0][:rows_out], g), _dot(mats[0][rows_out:], g)
        else:
            ar, ai = _dot(mats[0][...], g), _dot(mats[1][...], g)
        tr, ti = twiddle(rows(lambda i: twr_ref[i, blk(j), :], n1), rows(lambda i: twi_ref[i, blk(j), :], n1))
        br = (ar * tr - ai * ti).astype(BF16)
        bi = (ar * ti + ai * tr).astype(BF16)
        for i in range(n1):
            a_ref[0, i, blk(j), :] = br[i * r:(i + 1) * r]
            a_ref[1, i, blk(j), :] = bi[i * r:(i + 1) * r]

    return rows, twiddle, blk, strided_fwd


def _hyena_spectrum_kernel(hf_ref, hb_ref, m1_real_ref, f2_re_ref, f2_im_ref, twr_ref, twi_ref,
                           k_ref, a_ref, *, n1, n2, ct):
    h1 = n1 // 2
    rows, _, blk, strided_fwd = _fft_tools(n1, ct, twr_ref, twi_ref, a_ref)
    for part in range(2):
        def stage1(j, _):
            hf = rows(lambda i: hf_ref[i, blk(j), :], h1)
            hb = rows(lambda i: hb_ref[i, blk(j), :], h1)
            g = (hf + hb) if part == 0 else (hf - hb)
            strided_fwd((m1_real_ref,), g.astype(BF16), j)
            return 0
        lax.fori_loop(0, n2 // FFT_R, stage1, 0, unroll=FFT_UNROLL)

        def stage2(i, _):
            a = jnp.concatenate([a_ref[0, i], a_ref[1, i]], axis=0)
            f2 = f2_re_ref if part == 0 else f2_im_ref
            k_ref[part, i] = _dot(f2[...], a).astype(k_ref.dtype)
            return 0
        lax.fori_loop(0, n1, stage2, 0, unroll=FFT_UNROLL)


def _hyena_conv_kernel(u_ref, k_ref, d_ref, m1_re_ref, m1_im_ref, m1inv_re_ref, m1inv_im_ref,
                       f2_re_ref, f2_im_ref, f2conj_re_ref, f2conj_im_ref, twr_ref, twi_ref,
                       o_ref, a_ref, *, n1, n2, ct):
    r = FFT_R
    h1 = n1 // 2
    nblk = n2 // r
    rows, twiddle, blk, strided_fwd = _fft_tools(n1, ct, twr_ref, twi_ref, a_ref)

    def u_stage1(j, _):
        g = rows(lambda i: u_ref[i // h1, i % h1, blk(j), :], 2 * h1)
        strided_fwd((m1_re_ref, m1_im_ref), g, j)
        return 0
    lax.fori_loop(0, nblk, u_stage1, 0, unroll=FFT_UNROLL)

    def forward(i):
        a = jnp.concatenate([a_ref[0, i], a_ref[1, i]], axis=0)
        return _dot(f2_re_ref[...], a), _dot(f2_im_ref[...], a)

    def inverse(i, xr, xi):
        kr, ki = k_ref[0, i].astype(F32), k_ref[1, i].astype(F32)
        y = jnp.concatenate([(xr * kr - xi * ki).astype(BF16), (xr * ki + xi * kr).astype(BF16)], axis=0)
        br, bi = _dot(f2conj_re_ref[...], y), _dot(f2conj_im_ref[...], y)
        tr, ti = twiddle(twr_ref[i], twi_ref[i])
        a_ref[0, i] = (br * tr + bi * ti).astype(BF16)
        a_ref[1, i] = (bi * tr - br * ti).astype(BF16)

    def mid(g, _):
        base = g * FFT_UNROLL
        pending = forward(base)
        for j in range(FFT_UNROLL):
            nxt = forward(base + j + 1) if j + 1 < FFT_UNROLL else None
            inverse(base + j, *pending)
            pending = nxt
        return 0
    assert n1 % FFT_UNROLL == 0
    lax.fori_loop(0, n1 // FFT_UNROLL, mid, 0)

    def last(j, _):
        bb = rows(lambda i: a_ref[i // n1, i % n1, blk(j), :], 2 * n1)
        y = (_dot(m1inv_re_ref[...], bb), _dot(m1inv_im_ref[...], bb))
        d = d_ref[...]
        for bidx in range(2):
            for i in range(h1):
                u = u_ref[bidx, i, blk(j), :].astype(F32)
                o_ref[bidx, i, blk(j), :] = (y[bidx][i * r:(i + 1) * r] + u * d).astype(o_ref.dtype)
        return 0
    lax.fori_loop(0, nblk, last, 0, unroll=FFT_UNROLL)


def _hyena_conv(u, filt, hy_d, ct):
    b, l, c = u.shape
    assert b == 2, "the two batch rows are packed as real / imaginary parts of one complex transform"
    consts = _fft_constants(l)
    n1, n2 = consts["n1"], consts["n2"]
    h1 = n1 // 2
    nct = c // ct
    u4 = u.reshape(b, h1, n2, c)
    f3 = filt.reshape(h1, n2, 2 * c)
    mats = {k: jnp.asarray(v, dtype=BF16) for k, v in consts["mats"].items()}
    spec_consts = [mats[k] for k in ("m1_real", "f2_re", "f2_im")]
    conv_consts = [mats[k] for k in ("m1_re", "m1_im", "m1inv_re", "m1inv_im",
                                     "f2_re", "f2_im", "f2conj_re", "f2conj_im")]
    tws = [jnp.asarray(a) for a in consts["tw"]]
    const_spec = lambda a: pl.BlockSpec(a.shape, lambda ci: (0,) * a.ndim, pipeline_mode=pl.Buffered(1))
    spectrum = pl.pallas_call(
        functools.partial(_hyena_spectrum_kernel, n1=n1, n2=n2, ct=ct),
        grid=(nct,),
        in_specs=[
            pl.BlockSpec((h1, n2, ct), lambda ci: (0, 0, ci)),
            pl.BlockSpec((h1, n2, ct), lambda ci: (0, 0, nct + ci)),
        ] + [const_spec(a) for a in (*spec_consts, *tws)],
        out_specs=pl.BlockSpec((2, n1, n2, ct), lambda ci: (0, 0, 0, ci)),
        out_shape=jax.ShapeDtypeStruct((2, n1, n2, c), BF16),
        scratch_shapes=[pltpu.VMEM((2, n1, n2, ct), BF16)],
        compiler_params=_params("arbitrary"),
        name="hyena_filter_spectrum",
    )(f3, f3, *spec_consts, *tws)
    out = pl.pallas_call(
        functools.partial(_hyena_conv_kernel, n1=n1, n2=n2, ct=ct),
        grid=(nct,),
        in_specs=[
            pl.BlockSpec((b, h1, n2, ct), lambda ci: (0, 0, 0, ci)),
            pl.BlockSpec((2, n1, n2, ct), lambda ci: (0, 0, 0, ci)),
            pl.BlockSpec((1, ct), lambda ci: (0, ci)),
        ] + [const_spec(a) for a in (*conv_consts, *tws)],
        out_specs=pl.BlockSpec((b, h1, n2, ct), lambda ci: (0, 0, 0, ci)),
        out_shape=jax.ShapeDtypeStruct((b, h1, n2, c), BF16),
        scratch_shapes=[pltpu.VMEM((2, n1, n2, ct), BF16)],
        compiler_params=_params("arbitrary"),
        name="hyena_long_conv",
    )(u4, spectrum, hy_d, *conv_consts, *tws)
    return out.reshape(b * l, c)


def _merge_kernel(att_ref, yc_ref, x0_ref, ga_ref, gh_ref, x_ref, wa_ref, wh_ref, wo_ref, g_ref, x1_ref, hf_ref):
    a = _dot(att_ref[...], wa_ref[...])
    yhy = x0_ref[...] * yc_ref[...].astype(F32)
    h = _dot(yhy.astype(BF16), wh_ref[...])
    merged = jax.nn.sigmoid(ga_ref[...].astype(F32)) * a + jax.nn.sigmoid(gh_ref[...].astype(F32)) * h
    x1 = x_ref[...] + _dot(merged.astype(BF16), wo_ref[...])
    x1_ref[...] = x1
    ms = jnp.mean(x1 * x1, axis=-1, keepdims=True)
    hf_ref[...] = (x1 * lax.rsqrt(ms + NORM_EPS) * g_ref[...]).astype(BF16)


def _merge(att, yc, x0, gates, x2, wa, wh, wo, g_ffn, tm):
    m, d = x2.shape
    row = lambda w: pl.BlockSpec((tm, w), lambda i: (i, 0))
    const = lambda a: pl.BlockSpec(a.shape, lambda i: (0, 0), pipeline_mode=pl.Buffered(1))
    return pl.pallas_call(
        _merge_kernel,
        grid=(m // tm,),
        in_specs=[row(att.shape[1]), row(yc.shape[1]), row(x0.shape[1]),
                  pl.BlockSpec((tm, d), lambda i: (i, 0)), pl.BlockSpec((tm, d), lambda i: (i, 1)),
                  row(d), const(wa), const(wh), const(wo), const(g_ffn)],
        out_specs=[row(d), row(d)],
        out_shape=[jax.ShapeDtypeStruct((m, d), F32), jax.ShapeDtypeStruct((m, d), BF16)],
        compiler_params=_params("arbitrary"),
        name="merge_out_proj",
    )(att, yc, x0, gates, gates, x2, wa, wh, wo, g_ffn)


FFN_HALO = V7X_BF16_SUBLANES


def _ffn_kernel(hf_ref, prev_ref, next_ref, x1_ref, wg_ref, wv_ref, cwg_ref, cwv_ref, cbg_ref, cbv_ref,
                wd_ref, g_ref, o_ref, ext_ref, acc_ref, *, tm, tiles_per_seq):
    i = pl.program_id(0)
    f = pl.program_id(1)
    hl = FFN_HALO

    @pl.when(f == 0)
    def _():
        first = (i % tiles_per_seq) == 0
        last = (i % tiles_per_seq) == tiles_per_seq - 1
        ext_ref[0:hl, :] = jnp.where(first, jnp.zeros_like(prev_ref[...]), prev_ref[...])
        ext_ref[hl:hl + tm, :] = hf_ref[...]
        ext_ref[hl + tm:, :] = jnp.where(last, jnp.zeros_like(next_ref[...]), next_ref[...])
        acc_ref[...] = jnp.zeros_like(acc_ref)

    ext = ext_ref[...]
    ne = tm + 2 * hl

    def conv(w_ref, cw_ref, cb_ref):
        up = _dot(ext, w_ref[...])
        cw = cw_ref[...]
        prev = pltpu.roll(up, 1, 0)
        nxt = pltpu.roll(up, ne - 1, 0)
        full = prev * cw[0:1, :] + up * cw[1:2, :] + nxt * cw[2:3, :] + cb_ref[...]
        return full[hl:hl + tm]

    gate = conv(wg_ref, cwg_ref, cbg_ref)
    val = conv(wv_ref, cwv_ref, cbv_ref)
    act = (gate * jax.nn.sigmoid(gate) * val).astype(BF16)
    acc_ref[...] += _dot(act, wd_ref[...])

    @pl.when(f == pl.num_programs(1) - 1)
    def _():
        x2 = x1_ref[...] + acc_ref[...]
        ms = jnp.mean(x2 * x2, axis=-1, keepdims=True)
        o_ref[...] = x2 * lax.rsqrt(ms + NORM_EPS) * g_ref[...]


def _ffn(hf, x1, w_up, conv_w, conv_b, w_down, g_final, l, tm, tf):
    m, d = hf.shape
    dff = w_down.shape[0]
    nf = dff // tf
    hl = FFN_HALO
    assert l % tm == 0 and tm % hl == 0 and dff % tf == 0
    hb = tm // hl
    nhb = m // hl
    return pl.pallas_call(
        functools.partial(_ffn_kernel, tm=tm, tiles_per_seq=l // tm),
        grid=(m // tm, nf),
        in_specs=[
            pl.BlockSpec((tm, d), lambda i, f: (i, 0)),
            pl.BlockSpec((hl, d), lambda i, f: (jnp.maximum(i * hb - 1, 0), 0)),
            pl.BlockSpec((hl, d), lambda i, f: (jnp.minimum((i + 1) * hb, nhb - 1), 0)),
            pl.BlockSpec((tm, d), lambda i, f: (i, 0)),
            pl.BlockSpec((d, tf), lambda i, f: (0, f)),
            pl.BlockSpec((d, tf), lambda i, f: (0, nf + f)),
            pl.BlockSpec((3, tf), lambda i, f: (0, f)),
            pl.BlockSpec((3, tf), lambda i, f: (0, nf + f)),
            pl.BlockSpec((1, tf), lambda i, f: (0, f)),
            pl.BlockSpec((1, tf), lambda i, f: (0, nf + f)),
            pl.BlockSpec((tf, d), lambda i, f: (f, 0)),
            pl.BlockSpec((1, d), lambda i, f: (0, 0)),
        ],
        out_specs=pl.BlockSpec((tm, d), lambda i, f: (i, 0)),
        out_shape=jax.ShapeDtypeStruct((m, d), F32),
        scratch_shapes=[pltpu.VMEM((tm + 2 * hl, d), BF16), pltpu.VMEM((tm, d), F32)],
        compiler_params=_params("arbitrary", "arbitrary"),
        name="conv_ffn",
    )(hf, hf, hf, x1, w_up, w_up, conv_w, conv_w, conv_b, conv_b, w_down, g_final)


def _tile(n, want):
    t = min(n, want)
    assert n % t == 0
    return t


def kernel(x, g_mix, w_in, lambda_q1, lambda_k1, lambda_q2, lambda_k2, g_subln, rel_bias, hy_conv_w,
           hy_conv_b, hy_f_w1, hy_f_b1, hy_f_w2, hy_f_b2, hy_f_w3, hy_f_b3, hy_f_w4, hy_freq, hy_d,
           w_attn_branch, w_hyena_branch, w_out, g_ffn, w_up, ffn_conv_w, ffn_conv_b, w_down, g_final):
    b, l, d = x.shape
    m = b * l
    depth = w_in.shape[0]
    assert depth == 1
    qkv_cols = 3 * ATTN_HEADS * HEAD_COLS
    width = hy_d.shape[1]
    hy_cols = 3 * width
    gate_cols = 2 * d
    assert w_in.shape[2] == qkv_cols + hy_cols + gate_cols

    x2 = x.reshape(m, d)
    row = lambda v: v.reshape(1, -1)

    filt, w_in_b = _hyena_filter(l, hy_f_w1[0], hy_f_b1, hy_f_w2[0], hy_f_b2, hy_f_w3[0], hy_f_b3, hy_f_w4[0],
                                 hy_freq, _tile(l, 512), w_in[0])

    qkv, hyp, gates = _norm_proj(x2, g_mix, w_in_b, ((qkv_cols, BF16), (hy_cols, BF16), (gate_cols, BF16)),
                                 _tile(m, 1024), math.gcd(qkv_cols, hy_cols, gate_cols, 1024))

    t_attn = _tile(l, 256)
    bias_tiles = _bias_tiles(rel_bias, t_attn)
    lam_vecs = jnp.concatenate([lambda_q1, lambda_k1, lambda_q2, lambda_k2], axis=0)
    att, (wa_b, wh_b, wo_b, w_up_b, w_down_b) = _attention(
        qkv.reshape(b, l, qkv_cols), bias_tiles, lam_vecs, g_subln, t_attn,
        (w_attn_branch[0], w_hyena_branch[0], w_out[0], w_up[0], w_down[0]))
    att = att.reshape(m, ATTN_HEADS * HEAD_COLS)

    x0, u = _hy_gate(hyp.reshape(b, l, hy_cols), hy_conv_w[0], hy_conv_b, _tile(width, V7X_LANES))
    yc = _hyena_conv(u, filt, hy_d, _tile(width, V7X_MXU_DIM))

    x1, hf = _merge(att, yc, x0.reshape(m, width), gates, x2, wa_b, wh_b, wo_b, g_ffn, _tile(m, 512))

    out = _ffn(hf, x1, w_up_b, ffn_conv_w[0], ffn_conv_b, w_down_b,
               row(g_final), l, _tile(l, 512), _tile(w_down.shape[1], 512))
    return out.reshape(b, l, d)
```
